```python
import math, functools
import jax, jax.numpy as jnp
from jax import lax
import numpy as np

D_MODEL = 1024
BATCH = 16
SEQ = 4096
DEPTH = 1
DEC_BATCH = 128
DEC_SEQ = 8
PAST_LEN = 8192
PAGE_SIZE = 128

H_A = 4
DH = 64
D_ATTN = H_A * 2 * DH
D_CONV = D_MODEL - D_ATTN
CONV_W = 3
N_MEM = 256
H_MEM = 4
DH_MEM = D_MODEL // H_MEM
D_FF = 4 * D_MODEL
D_IN = 3 * D_ATTN + 3 * D_CONV
ROPE_THETA = 10000.0
EPS = 1e-6
Q_BLOCK = 128
ATTN_SCALE = DH ** -0.5
MEM_SCALE = DH_MEM ** -0.5

kernel_name = "hymba_diffattn_shortconv_decoder_step"


def lambda_init(l):
    return 0.8 - 0.6 * math.exp(-0.3 * l)


def rms_norm(x, g):
    xf = x.astype(jnp.float32)
    y = xf * lax.rsqrt(jnp.mean(xf * xf, axis=-1, keepdims=True) + EPS)
    return (y * g.astype(jnp.float32)).astype(x.dtype)


def rope(x, pos):
    d = x.shape[-1]
    half = d // 2
    inv = jnp.exp(jnp.arange(half, dtype=jnp.float32) * (-2.0 * math.log(ROPE_THETA) / d))
    ang = pos.astype(jnp.float32)[:, None] * inv[None, :]
    cos = jnp.cos(ang)[None, :, None, :]
    sin = jnp.sin(ang)[None, :, None, :]
    xf = x.astype(jnp.float32)
    x1, x2 = xf[..., :half], xf[..., half:]
    return jnp.concatenate([x1 * cos - x2 * sin, x2 * cos + x1 * sin], axis=-1).astype(x.dtype)


def diff_lambda(lq1, lk1, lq2, lk2, lam0):
    f = lambda a: a.astype(jnp.float32)
    return jnp.exp(jnp.sum(f(lq1) * f(lk1))) - jnp.exp(jnp.sum(f(lq2) * f(lk2))) + lam0


def diff_combine(p, lam):
    b, _, tq, tk = p.shape
    p = p.reshape(b, H_A, 2, tq, tk)
    return p[:, :, 0] - lam * p[:, :, 1]


def diff_attn_prompt(q, k, v, lam):
    b, s_len = q.shape[:2]
    nb = s_len // Q_BLOCK
    qb = q.reshape(b, nb, Q_BLOCK, 2 * H_A, DH).swapaxes(0, 1)
    kpos = jnp.arange(s_len)

    def block(args):
        qi, i = args
        s = jnp.einsum('bqhd,bkhd->bhqk', qi, k).astype(jnp.float32) * ATTN_SCALE
        qpos = i * Q_BLOCK + jnp.arange(Q_BLOCK)
        s = jnp.where(kpos[None, :] <= qpos[:, None], s, -jnp.inf)
        a = diff_combine(jax.nn.softmax(s, axis=-1), lam)
        return jnp.einsum('bhqk,bkhd->bqhd', a.astype(v.dtype), v)

    o = lax.map(block, (qb, jnp.arange(nb)))
    return o.swapaxes(0, 1).reshape(b, s_len, H_A, 2 * DH)


def diff_attn_sample(q, k_new, v_new, lam, pool_k, pool_v, page_table):
    b, t = q.shape[:2]
    k_past = pool_k[page_table].reshape(b, -1, 2 * H_A, DH)
    v_past = pool_v[page_table].reshape(b, -1, H_A, 2 * DH)
    p_len = k_past.shape[1]
    s_past = jnp.einsum('bqhd,bkhd->bhqk', q, k_past).astype(jnp.float32) * ATTN_SCALE
    s_new = jnp.einsum('bqhd,bkhd->bhqk', q, k_new).astype(jnp.float32) * ATTN_SCALE
    causal = jnp.tril(jnp.ones((t, t), dtype=bool))
    s_new = jnp.where(causal, s_new, -jnp.inf)
    p = jax.nn.softmax(jnp.concatenate([s_past, s_new], axis=-1), axis=-1)
    a_past = diff_combine(p[..., :p_len], lam).astype(v_new.dtype)
    a_new = diff_combine(p[..., p_len:], lam).astype(v_new.dtype)
    return (jnp.einsum('bhqk,bkhd->bqhd', a_past, v_past)
            + jnp.einsum('bhqk,bkhd->bqhd', a_new, v_new))


def short_conv(u, prev, w):
    t = u.shape[1]
    up = jnp.concatenate([prev, u], axis=1)
    y = w[0] * up[:, 0:t]
    for j in range(1, CONV_W):
        y = y + w[j] * up[:, j:j + t]
    return y, up[:, -(CONV_W - 1):]


def parallel_mixer(xn, pos, attend, conv_prev, w_in, lam, lam0, g_subln, conv_w, g_conv, w_out):
    b, t, _ = xn.shape
    p = xn @ w_in
    q = p[..., 0:D_ATTN].reshape(b, t, 2 * H_A, DH)
    k = p[..., D_ATTN:2 * D_ATTN].reshape(b, t, 2 * H_A, DH)
    v = p[..., 2 * D_ATTN:3 * D_ATTN].reshape(b, t, H_A, 2 * DH)
    o0 = 3 * D_ATTN
    gate_b = p[..., o0:o0 + D_CONV]
    gate_c = p[..., o0 + D_CONV:o0 + 2 * D_CONV]
    h = p[..., o0 + 2 * D_CONV:o0 + 3 * D_CONV]
    q = rope(q, pos)
    k = rope(k, pos)
    o = attend(q, k, v, lam)
    o = (rms_norm(o, g_subln) * (1.0 - lam0)).reshape(b, t, D_ATTN)
    c, conv_state = short_conv(gate_c * h, conv_prev, conv_w)
    c = rms_norm(gate_b * c, g_conv)
    y = jnp.concatenate([o, c], axis=-1) @ w_out
    return y, k, v, conv_state


def memory_kv(mem_n, w_k, w_v):
    b, m, _ = mem_n.shape
    return ((mem_n @ w_k).reshape(b, m, H_MEM, DH_MEM),
            (mem_n @ w_v).reshape(b, m, H_MEM, DH_MEM))


def cross_attn(xn, mk, mv, w_q, w_o):
    b, t, _ = xn.shape
    q = (xn @ w_q).reshape(b, t, H_MEM, DH_MEM)
    s = jnp.einsum('bqhd,bkhd->bhqk', q, mk).astype(jnp.float32) * MEM_SCALE
    p = jax.nn.softmax(s, axis=-1).astype(mv.dtype)
    o = jnp.einsum('bhqk,bkhd->bqhd', p, mv).reshape(b, t, D_MODEL)
    return o @ w_o


def squared_relu_mlp(xn, w_up, w_down):
    h = jax.nn.relu(xn @ w_up)
    return (h * h) @ w_down


def setup_inputs(seed: int = 0) -> dict:
    key = jax.random.key(seed)
    ks = jax.random.split(key, 32)
    f32 = jnp.float32
    n_pages = PAST_LEN // PAGE_SIZE
    n_pool = (DEC_BATCH * n_pages * 5 + 3) // 4

    def dense(k, shape, fan_in):
        return jax.random.normal(k, shape, f32) * (fan_in ** -0.5)

    def gain(k, shape):
        return 1.0 + 0.1 * jax.random.normal(k, shape, f32)

    page_table = jax.random.permutation(ks[9], n_pool)[:DEC_BATCH * n_pages]
    page_table = page_table.reshape(DEC_BATCH, n_pages).astype(jnp.int32)
    return {
        "x_prompt": jax.random.normal(ks[0], (BATCH, SEQ, D_MODEL), f32),
        "x_sample": jax.random.normal(ks[1], (DEC_BATCH, DEC_SEQ, D_MODEL), f32),
        "mem_prompt": jax.random.normal(ks[2], (BATCH, N_MEM, D_MODEL), f32),
        "cache_k": jax.random.normal(ks[3], (DEPTH, n_pool, PAGE_SIZE, 2 * H_A, DH), f32),
        "cache_v": jax.random.normal(ks[4], (DEPTH, n_pool, PAGE_SIZE, H_A, 2 * DH), f32),
        "state_conv": jax.random.normal(ks[5], (DEPTH, DEC_BATCH, CONV_W - 1, D_CONV), f32),
        "cache_mem_k": jax.random.normal(ks[6], (DEPTH, DEC_BATCH, N_MEM, H_MEM, DH_MEM), f32),
        "cache_mem_v": jax.random.normal(ks[7], (DEPTH, DEC_BATCH, N_MEM, H_MEM, DH_MEM), f32),
        "page_table": page_table,
        "g_mix": gain(ks[10], (DEPTH, D_MODEL)),
        "w_in": dense(ks[11], (DEPTH, D_MODEL, D_IN), D_MODEL),
        "lambda_q1": 0.1 * jax.random.normal(ks[12], (DEPTH, DH), f32),
        "lambda_k1": 0.1 * jax.random.normal(ks[13], (DEPTH, DH), f32),
        "lambda_q2": 0.1 * jax.random.normal(ks[14], (DEPTH, DH), f32),
        "lambda_k2": 0.1 * jax.random.normal(ks[15], (DEPTH, DH), f32),
        "g_subln": gain(ks[16], (DEPTH, 2 * DH)),
        "conv_w": dense(ks[17], (DEPTH, CONV_W, D_CONV), CONV_W),
        "g_conv": gain(ks[18], (DEPTH, D_CONV)),
        "w_out": dense(ks[19], (DEPTH, D_MODEL, D_MODEL), D_MODEL),
        "g_cross": gain(ks[20], (DEPTH, D_MODEL)),
        "g_mem": gain(ks[21], (DEPTH, D_MODEL)),
        "w_q_mem": dense(ks[22], (DEPTH, D_MODEL, D_MODEL), D_MODEL),
        "w_k_mem": dense(ks[23], (DEPTH, D_MODEL, D_MODEL), D_MODEL),
        "w_v_mem": dense(ks[24], (DEPTH, D_MODEL, D_MODEL), D_MODEL),
        "w_o_mem": dense(ks[25], (DEPTH, D_MODEL, D_MODEL), D_MODEL),
        "g_mlp": gain(ks[26], (DEPTH, D_MODEL)),
        "w_up": dense(ks[27], (DEPTH, D_MODEL, D_FF), D_MODEL),
        "w_down": dense(ks[28], (DEPTH, D_FF, D_MODEL), D_FF),
        "g_final": gain(ks[29], (D_MODEL,)),
    }


def reference(x_prompt, x_sample, mem_prompt, cache_k, cache_v, state_conv, cache_mem_k,
              cache_mem_v, page_table, g_mix, w_in, lambda_q1, lambda_k1, lambda_q2, lambda_k2,
              g_subln, conv_w, g_conv, w_out, g_cross, g_mem, w_q_mem, w_k_mem, w_v_mem,
              w_o_mem, g_mlp, w_up, w_down, g_final):
    def run_layer(x, l, pos, attend, conv_prev, mk, mv):
        lam0 = lambda_init(l)
        lam = diff_lambda(lambda_q1[l], lambda_k1[l], lambda_q2[l], lambda_k2[l], lam0)
        h, k, v, cs = parallel_mixer(rms_norm(x, g_mix[l]), pos, attend, conv_prev, w_in[l], lam,
                                     lam0, g_subln[l], conv_w[l], g_conv[l], w_out[l])
        x = x + h
        x = x + cross_attn(rms_norm(x, g_cross[l]), mk, mv, w_q_mem[l], w_o_mem[l])
        x = x + squared_relu_mlp(rms_norm(x, g_mlp[l]), w_up[l], w_down[l])
        return x, k, v, cs

    b_p, s_p = x_prompt.shape[:2]
    pos_p = jnp.arange(s_p)
    zero_prev = jnp.zeros((b_p, CONV_W - 1, D_CONV), x_prompt.dtype)
    xp = x_prompt
    kp, vp, cp, mkp, mvp = [], [], [], [], []
    for l in range(DEPTH):
        mk, mv = memory_kv(rms_norm(mem_prompt, g_mem[l]), w_k_mem[l], w_v_mem[l])
        xp, k, v, cs = run_layer(xp, l, pos_p, diff_attn_prompt, zero_prev, mk, mv)
        kp.append(k); vp.append(v); cp.append(cs); mkp.append(mk); mvp.append(mv)
    y_prompt = rms_norm(xp, g_final)

    past_len = page_table.shape[1] * cache_k.shape[2]
    pos_s = past_len + jnp.arange(x_sample.shape[1])
    xs = x_sample
    ks_, vs_, cs_ = [], [], []
    for l in range(DEPTH):
        attend = functools.partial(diff_attn_sample, pool_k=cache_k[l], pool_v=cache_v[l],
                                   page_table=page_table)
        xs, k, v, cs = run_layer(xs, l, pos_s, attend, state_conv[l], cache_mem_k[l],
                                 cache_mem_v[l])
        ks_.append(k); vs_.append(v); cs_.append(cs)
    y_sample = rms_norm(xs, g_final)

    return (y_prompt, y_sample, jnp.stack(kp), jnp.stack(vp), jnp.stack(cp), jnp.stack(mkp),
            jnp.stack(mvp), jnp.stack(ks_), jnp.stack(vs_), jnp.stack(cs_))
```

```python
import functools
import math

import jax
import jax.numpy as jnp
from jax import lax
from jax.experimental import pallas as pl
from jax.experimental.pallas import tpu as pltpu

F32 = jnp.float32
BF16 = jnp.bfloat16

H_A = 4
DH = 64
N_SUB = 2 * H_A
HEAD_W = 2 * DH
D_ATTN = H_A * HEAD_W
CONV_W = 3
H_MEM = 4
ROPE_THETA = 10000.0
EPS = 1e-6
ATTN_SCALE = DH ** -0.5
LANES = 128
SUBLANES = 8
VMEM_LIMIT = 56 * 1024 * 1024

ROW_TILE = 512
PAGES_PER_STEP = 8


def _lambda_init(l):
    return 0.8 - 0.6 * math.exp(-0.3 * l)


def _rms(x, g):
    return x * lax.rsqrt(jnp.mean(x * x, axis=-1, keepdims=True) + EPS) * g


def _dot(a, b):
    return jnp.dot(a, b, preferred_element_type=F32)


def _dot_nt(a, b):
    return lax.dot_general(a, b, (((1,), (1,)), ((), ())), preferred_element_type=F32)


def _params(n_axes):
    return pltpu.CompilerParams(dimension_semantics=("arbitrary",) * n_axes,
                                vmem_limit_bytes=VMEM_LIMIT)


def _diff_lambda(lq1_ref, lk1_ref, lq2_ref, lk2_ref, lam0):
    a = jnp.sum(lq1_ref[...] * lk1_ref[...], axis=-1, keepdims=True)
    b = jnp.sum(lq2_ref[...] * lk2_ref[...], axis=-1, keepdims=True)
    return jnp.exp(a) - jnp.exp(b) + lam0


def _store_heads(ref, x):
    w = ref.shape[2]
    for h in range(ref.shape[1]):
        ref[:, h, :] = x[:, h * w:(h + 1) * w].astype(ref.dtype)


def _mem_kv_kernel(m_ref, g_ref, wk_ref, wv_ref, k_ref, v_ref, kb_ref, vb_ref):
    mn = _rms(m_ref[...], g_ref[...]).astype(BF16)
    k = _dot(mn, wk_ref[...])
    v = _dot(mn, wv_ref[...])
    _store_heads(k_ref, k)
    _store_heads(v_ref, v)
    kb_ref[...] = k.astype(BF16)
    vb_ref[...] = v.astype(BF16)


def _mem_kv(mem, g, wk, wv):
    n, d = mem.shape
    tm = min(ROW_TILE, n)
    row = lambda i: (i, 0)
    fixed = lambda i: (0, 0)
    heads = pl.BlockSpec((tm, H_MEM, d // H_MEM), lambda i: (i, 0, 0))
    heads_shape = jax.ShapeDtypeStruct((n, H_MEM, d // H_MEM), F32)
    return pl.pallas_call(
        _mem_kv_kernel,
        grid=(n // tm,),
        in_specs=[pl.BlockSpec((tm, d), row), pl.BlockSpec((1, d), fixed),
                  pl.BlockSpec((d, d), fixed), pl.BlockSpec((d, d), fixed)],
        out_specs=[heads, heads, pl.BlockSpec((tm, d), row), pl.BlockSpec((tm, d), row)],
        out_shape=[heads_shape, heads_shape] + [jax.ShapeDtypeStruct((n, d), BF16)] * 2,
        compiler_params=_params(1),
        name="mem_kv",
    )(mem, g, wk, wv)


def _rope_rows(p, cos, sin):
    lane = lax.broadcasted_iota(jnp.int32, (p.shape[0], LANES), 1)
    first_half = (lane % DH) < (DH // 2)
    out = []
    for c in range(D_ATTN // LANES):
        pc = p[:, c * LANES:(c + 1) * LANES]
        swapped = jnp.where(first_half, pltpu.roll(pc, LANES - DH // 2, 1),
                            pltpu.roll(pc, DH // 2, 1))
        out.append(pc * cos + swapped * sin)
    return out


def _conv_branch(xn, w_ref, cw_ref, gc_ref, fix_history):
    dc = cw_ref.shape[1]
    o0 = 3 * D_ATTN
    gate_b = _dot(xn, w_ref[:, o0:o0 + dc])
    u = _dot(xn, w_ref[:, o0 + dc:o0 + 2 * dc]) * _dot(xn, w_ref[:, o0 + 2 * dc:o0 + 3 * dc])
    row = lax.broadcasted_iota(jnp.int32, u.shape, 0)
    um1, um2 = fix_history(row, pltpu.roll(u, 1, 0), pltpu.roll(u, 2, 0))
    y = cw_ref[0:1, :] * um2 + cw_ref[1:2, :] * um1 + cw_ref[2:3, :] * u
    return _rms(gate_b * y, gc_ref[...]).astype(BF16), u


def _mixer_in_prompt_kernel(x_ref, g_ref, w_ref, wkt_ref, cos_ref, sin_ref, cost_ref, sint_ref,
                            cw_ref, gc_ref, q_ref, kt_ref, ktb_ref, v_ref, vb_ref, c_ref, cs_ref,
                            carry_ref, *, tiles_per_seq):
    tm = x_ref.shape[0]
    xn = _rms(x_ref[...], g_ref[...]).astype(BF16)

    q = _rope_rows(_dot(xn, w_ref[:, 0:D_ATTN]), cos_ref[...], sin_ref[...])
    for c, r in enumerate(q):
        q_ref[:, c * LANES:(c + 1) * LANES] = (r * ATTN_SCALE).astype(q_ref.dtype)

    kt = _dot_nt(wkt_ref[...], xn)
    cos_t = cost_ref[...]
    sin_t = sint_ref[...]
    half = DH // 2
    for s in range(N_SUB):
        x1 = kt[s * DH:s * DH + half, :]
        x2 = kt[s * DH + half:(s + 1) * DH, :]
        r1 = x1 * cos_t - x2 * sin_t
        r2 = x2 * cos_t + x1 * sin_t
        for ref in (kt_ref, ktb_ref):
            ref[s * DH:s * DH + half, :] = r1.astype(ref.dtype)
            ref[s * DH + half:(s + 1) * DH, :] = r2.astype(ref.dtype)

    v = _dot(xn, w_ref[:, 2 * D_ATTN:3 * D_ATTN])
    _store_heads(v_ref, v)
    vb_ref[...] = v.astype(BF16)

    @pl.when(pl.program_id(0) % tiles_per_seq == 0)
    def _():
        carry_ref[...] = jnp.zeros_like(carry_ref)

    def fix_history(row, um1, um2):
        prev0 = carry_ref[SUBLANES - 2:SUBLANES - 1, :]
        prev1 = carry_ref[SUBLANES - 1:SUBLANES, :]
        return (jnp.where(row == 0, prev1, um1),
                jnp.where(row == 0, prev0, jnp.where(row == 1, prev1, um2)))

    c, u = _conv_branch(xn, w_ref, cw_ref, gc_ref, fix_history)
    c_ref[...] = c
    carry_ref[...] = u[tm - SUBLANES:tm, :]
    cs_ref[...] = u[tm - (CONV_W - 1):tm, :]


def _mixer_in_prompt(x, g, w_in, w_kt, tabs, conv_w, g_conv, *, batch, seq):
    n, d = x.shape
    dc = conv_w.shape[1]
    tm = min(ROW_TILE, seq)
    assert seq % tm == 0
    tps = seq // tm
    cos, sin, cos_t, sin_t = tabs
    row = lambda i: (i, 0)
    fixed = lambda i: (0, 0)
    return pl.pallas_call(
        functools.partial(_mixer_in_prompt_kernel, tiles_per_seq=tps),
        grid=(n // tm,),
        in_specs=[pl.BlockSpec((tm, d), row), pl.BlockSpec((1, d), fixed),
                  pl.BlockSpec(w_in.shape, fixed), pl.BlockSpec(w_kt.shape, fixed),
                  pl.BlockSpec((tm, LANES), lambda i: (i % tps, 0)),
                  pl.BlockSpec((tm, LANES), lambda i: (i % tps, 0)),
                  pl.BlockSpec((DH // 2, tm), lambda i: (0, i % tps)),
                  pl.BlockSpec((DH // 2, tm), lambda i: (0, i % tps)),
                  pl.BlockSpec((CONV_W, dc), fixed), pl.BlockSpec((1, dc), fixed)],
        out_specs=[pl.BlockSpec((tm, D_ATTN), row),
                   pl.BlockSpec((None, D_ATTN, tm), lambda i: (i // tps, 0, i % tps)),
                   pl.BlockSpec((None, None, D_ATTN, tm), lambda i: (i // tps, i % tps, 0, 0)),
                   pl.BlockSpec((tm, H_A, HEAD_W), lambda i: (i, 0, 0)),
                   pl.BlockSpec((tm, D_ATTN), row),
                   pl.BlockSpec((tm, dc), row),
                   pl.BlockSpec((None, CONV_W - 1, dc), lambda i: (i // tps, 0, 0))],
        out_shape=[jax.ShapeDtypeStruct((n, D_ATTN), BF16),
                   jax.ShapeDtypeStruct((batch, D_ATTN, seq), F32),
                   jax.ShapeDtypeStruct((batch, tps, D_ATTN, tm), BF16),
                   jax.ShapeDtypeStruct((n, H_A, HEAD_W), F32),
                   jax.ShapeDtypeStruct((n, D_ATTN), BF16),
                   jax.ShapeDtypeStruct((n, dc), BF16),
                   jax.ShapeDtypeStruct((batch, CONV_W - 1, dc), F32)],
        scratch_shapes=[pltpu.VMEM((SUBLANES, dc), F32)],
        compiler_params=_params(1),
        name="mixer_in_prompt",
    )(x, g, w_in, w_kt, cos, sin, cos_t, sin_t, conv_w, g_conv)


def _mixer_in_sample_kernel(x_ref, g_ref, w_ref, cos_ref, sin_ref, cw_ref, gc_ref, p1_ref, p2_ref,
                            q_ref, k_ref, v_ref, c_ref, u_ref, *, seq_rows):
    xn = _rms(x_ref[...], g_ref[...]).astype(BF16)
    cos = cos_ref[...]
    sin = sin_ref[...]
    for c, r in enumerate(_rope_rows(_dot(xn, w_ref[:, 0:D_ATTN]), cos, sin)):
        q_ref[:, c * LANES:(c + 1) * LANES] = r * ATTN_SCALE
    for c, r in enumerate(_rope_rows(_dot(xn, w_ref[:, D_ATTN:2 * D_ATTN]), cos, sin)):
        k_ref[:, c * LANES:(c + 1) * LANES] = r
    v_ref[...] = _dot(xn, w_ref[:, 2 * D_ATTN:3 * D_ATTN])

    def fix_history(row, um1, um2):
        t = row % seq_rows
        return jnp.where(t == 0, p1_ref[...], um1), jnp.where(t < 2, p2_ref[...], um2)

    c, u = _conv_branch(xn, w_ref, cw_ref, gc_ref, fix_history)
    c_ref[...] = c
    u_ref[...] = u


def _mixer_in_sample(x, g, w_in, tabs, conv_w, g_conv, hist, *, seq_rows):
    n, d = x.shape
    dc = conv_w.shape[1]
    tm = min(ROW_TILE, n)
    cos, sin = tabs
    row = lambda i: (i, 0)
    fixed = lambda i: (0, 0)
    act = pl.BlockSpec((tm, D_ATTN), row)
    conv = pl.BlockSpec((tm, dc), row)
    return pl.pallas_call(
        functools.partial(_mixer_in_sample_kernel, seq_rows=seq_rows),
        grid=(n // tm,),
        in_specs=[pl.BlockSpec((tm, d), row), pl.BlockSpec((1, d), fixed),
                  pl.BlockSpec(w_in.shape, fixed),
                  pl.BlockSpec((tm, LANES), fixed), pl.BlockSpec((tm, LANES), fixed),
                  pl.BlockSpec((CONV_W, dc), fixed), pl.BlockSpec((1, dc), fixed), conv, conv],
        out_specs=[act, act, act, conv, conv],
        out_shape=[jax.ShapeDtypeStruct((n, D_ATTN), F32)] * 3
                  + [jax.ShapeDtypeStruct((n, dc), BF16), jax.ShapeDtypeStruct((n, dc), F32)],
        compiler_params=_params(1),
        name="mixer_in_sample",
    )(x, g, w_in, cos, sin, conv_w, g_conv, *hist)


def _softmax_step(s, m, l, acc, v):
    m_new = jnp.maximum(m, jnp.max(s, axis=-1, keepdims=True))
    p = jnp.exp(s - m_new)
    alpha = jnp.exp(m - m_new)
    l = alpha * l + jnp.sum(p, axis=-1, keepdims=True)
    acc = alpha * acc + _dot(p.astype(BF16), v)
    return m_new, l, acc


def _head_out(a0, l0, a1, l1, lam, g, lam0):
    o = a0 / l0 - lam * (a1 / l1)
    return _rms(o, g) * (1.0 - lam0)


def _prompt_attn_kernel(q_ref, kt_ref, v_ref, lq1, lk1, lq2, lk2, gs_ref, o_ref, *, lam0):
    tq = q_ref.shape[0]
    qi = pl.program_id(2)
    q = q_ref[...]
    lane = lax.broadcasted_iota(jnp.int32, q.shape, 1)
    q0 = jnp.where(lane < DH, q, jnp.zeros_like(q))
    q1 = jnp.where(lane >= DH, q, jnp.zeros_like(q))

    def step(j, carry, masked):
        m0, l0, a0, m1, l1, a1 = carry
        kt = kt_ref[j]
        v = v_ref[pl.ds(pl.multiple_of(j * tq, tq), tq), :]
        s0 = _dot(q0, kt)
        s1 = _dot(q1, kt)
        if masked:
            keep = (lax.broadcasted_iota(jnp.int32, s0.shape, 1)
                    <= lax.broadcasted_iota(jnp.int32, s0.shape, 0))
            s0 = jnp.where(keep, s0, -jnp.inf)
            s1 = jnp.where(keep, s1, -jnp.inf)
        m0, l0, a0 = _softmax_step(s0, m0, l0, a0, v)
        m1, l1, a1 = _softmax_step(s1, m1, l1, a1, v)
        return m0, l0, a0, m1, l1, a1

    neg = jnp.full((tq, 1), -jnp.inf, F32)
    zero = jnp.zeros((tq, 1), F32)
    zacc = jnp.zeros((tq, HEAD_W), F32)
    carry = (neg, zero, zacc, neg, zero, zacc)
    carry = lax.fori_loop(0, qi, lambda j, c: step(j, c, False), carry)
    m0, l0, a0, m1, l1, a1 = step(qi, carry, True)
    lam = _diff_lambda(lq1, lk1, lq2, lk2, lam0)
    o_ref[...] = _head_out(a0, l0, a1, l1, lam, gs_ref[...], lam0).astype(o_ref.dtype)


def _prompt_attn(q, ktb, vb, lams, g_subln, *, lam0):
    n = q.shape[0]
    batch, nk, _, tk = ktb.shape
    seq = nk * tk
    vec = pl.BlockSpec((1, DH), lambda b, h, i: (0, 0))
    return pl.pallas_call(
        functools.partial(_prompt_attn_kernel, lam0=lam0),
        grid=(batch, H_A, nk),
        in_specs=[pl.BlockSpec((tk, HEAD_W), lambda b, h, i: (b * nk + i, h)),
                  pl.BlockSpec((None, nk, HEAD_W, tk), lambda b, h, i: (b, 0, h, 0)),
                  pl.BlockSpec((seq, HEAD_W), lambda b, h, i: (b, h)),
                  vec, vec, vec, vec,
                  pl.BlockSpec((1, HEAD_W), lambda b, h, i: (0, 0))],
        out_specs=pl.BlockSpec((tk, HEAD_W), lambda b, h, i: (b * nk + i, h)),
        out_shape=jax.ShapeDtypeStruct((n, D_ATTN), BF16),
        compiler_params=_params(3),
        name="prompt_attn",
    )(q, ktb, vb, *lams, g_subln)


def _paged_attn_kernel(pt_ref, q_ref, kn_ref, vn_ref, lq1, lk1, lq2, lk2, gs_ref, *rest,
                       pages, lam0):
    del pt_ref
    k_pages = rest[:pages]
    v_pages = rest[pages:2 * pages]
    o_ref, m_ref, l_ref, acc_ref = rest[2 * pages:]
    t, dq = q_ref.shape
    rows = N_SUB * t
    c = pl.program_id(1)

    q = q_ref[...]
    qt = jnp.concatenate([q] * N_SUB, axis=0)
    r_i = lax.broadcasted_iota(jnp.int32, (rows, dq), 0)
    c_i = lax.broadcasted_iota(jnp.int32, (rows, dq), 1)
    qbd = jnp.where((r_i // t) == (c_i // DH), qt, jnp.zeros_like(qt)).astype(BF16)

    @pl.when(c == 0)
    def _():
        m_ref[...] = jnp.full_like(m_ref, -jnp.inf)
        l_ref[...] = jnp.zeros_like(l_ref)
        acc_ref[...] = jnp.zeros_like(acc_ref)

    def update(scores, values):
        m_prev = m_ref[...]
        m_new = m_prev
        for s in scores:
            m_new = jnp.maximum(m_new, jnp.max(s, axis=-1, keepdims=True))
        alpha = jnp.exp(m_prev - m_new)
        l_new = alpha * l_ref[...]
        acc = alpha * acc_ref[...]
        for s, value in zip(scores, values):
            p = jnp.exp(s - m_new)
            l_new = l_new + jnp.sum(p, axis=-1, keepdims=True)
            pb = p.astype(BF16)
            acc = acc + jnp.concatenate(
                [_dot(pb[2 * h * t:2 * (h + 1) * t, :], value(h)) for h in range(H_A)], axis=0)
        m_ref[...] = m_new
        l_ref[...] = l_new
        acc_ref[...] = acc

    scores = [_dot(qbd, k_pages[j][...].reshape(dq, -1).astype(BF16)) for j in range(pages)]
    values = [functools.partial(lambda h, j: v_pages[j][:, h, :].astype(BF16), j=j)
              for j in range(pages)]
    update(scores, values)

    @pl.when(c == pl.num_programs(1) - 1)
    def _():
        pad = jnp.zeros((LANES - t, dq), F32)
        kn = jnp.concatenate([kn_ref[...], pad], axis=0).astype(BF16)
        vn = jnp.concatenate([vn_ref[...], pad], axis=0).astype(BF16)
        s = _dot_nt(qbd, kn)
        key_t = lax.broadcasted_iota(jnp.int32, s.shape, 1)
        qry_t = lax.broadcasted_iota(jnp.int32, s.shape, 0) % t
        update([jnp.where(key_t <= qry_t, s, -jnp.inf)],
               [lambda h: vn[:, h * HEAD_W:(h + 1) * HEAD_W]])
        lam = _diff_lambda(lq1, lk1, lq2, lk2, lam0)
        acc = acc_ref[...]
        l = l_ref[...]
        for h in range(H_A):
            r0 = 2 * h * t
            o = _head_out(acc[r0:r0 + t], l[r0:r0 + t], acc[r0 + t:r0 + 2 * t],
                          l[r0 + t:r0 + 2 * t], lam, gs_ref[...], lam0)
            o_ref[:, h * HEAD_W:(h + 1) * HEAD_W] = o.astype(o_ref.dtype)


def _paged_attn(q, k_new, v_new, lams, g_subln, pool_kt, pool_v, page_table, *, lam0):
    n_seq, n_pages = page_table.shape
    n = q.shape[0]
    t = n // n_seq
    page = pool_v.shape[1]
    pages = min(PAGES_PER_STEP, n_pages)
    assert n_pages % pages == 0
    seq = pl.BlockSpec((t, D_ATTN), lambda b, c, pt: (b, 0))
    vec = pl.BlockSpec((1, DH), lambda b, c, pt: (0, 0))

    def page_index(j):
        return lambda b, c, pt: (pt[b * n_pages + c * pages + j], 0, 0, 0)

    rows = N_SUB * t
    grid_spec = pltpu.PrefetchScalarGridSpec(
        num_scalar_prefetch=1,
        grid=(n_seq, n_pages // pages),
        in_specs=[seq, seq, seq, vec, vec, vec, vec,
                  pl.BlockSpec((1, HEAD_W), lambda b, c, pt: (0, 0))]
                 + [pl.BlockSpec((None, N_SUB, DH, page), page_index(j)) for j in range(pages)]
                 + [pl.BlockSpec((None, page, H_A, HEAD_W), page_index(j)) for j in range(pages)],
        out_specs=seq,
        scratch_shapes=[pltpu.VMEM((rows, 1), F32), pltpu.VMEM((rows, 1), F32),
                        pltpu.VMEM((rows, HEAD_W), F32)],
    )
    return pl.pallas_call(
        functools.partial(_paged_attn_kernel, pages=pages, lam0=lam0),
        grid_spec=grid_spec,
        out_shape=jax.ShapeDtypeStruct((n, D_ATTN), F32),
        compiler_params=_params(2),
        name="paged_attn",
    )(page_table.reshape(-1), q, k_new, v_new, *lams, g_subln,
      *([pool_kt] * pages), *([pool_v] * pages))


def _mixer_out_kernel(x_ref, o_ref, c_ref, wo_ref, g_ref, wq_ref, x1_ref, qm_ref, *, scale):
    da = o_ref.shape[1]
    x1 = (x_ref[...] + _dot(o_ref[...].astype(BF16), wo_ref[0:da, :])
          + _dot(c_ref[...], wo_ref[da:, :]))
    x1_ref[...] = x1
    xn = _rms(x1, g_ref[...]).astype(BF16)
    qm_ref[...] = (_dot(xn, wq_ref[...]) * scale).astype(qm_ref.dtype)


def _mixer_out(x, o, c, w_out, g_cross, w_q, *, scale, act_dtype):
    n, d = x.shape
    tm = min(ROW_TILE, n)
    row = lambda i: (i, 0)
    fixed = lambda i: (0, 0)
    return pl.pallas_call(
        functools.partial(_mixer_out_kernel, scale=scale),
        grid=(n // tm,),
        in_specs=[pl.BlockSpec((tm, d), row), pl.BlockSpec((tm, o.shape[1]), row),
                  pl.BlockSpec((tm, c.shape[1]), row), pl.BlockSpec((d, d), fixed),
                  pl.BlockSpec((1, d), fixed), pl.BlockSpec((d, d), fixed)],
        out_specs=[pl.BlockSpec((tm, d), row)] * 2,
        out_shape=[jax.ShapeDtypeStruct((n, d), F32), jax.ShapeDtypeStruct((n, d), act_dtype)],
        compiler_params=_params(1),
        name="mixer_out",
    )(x, o, c, w_out, g_cross, w_q)


def _cross_attn_kernel(q_ref, mk_ref, mv_ref, o_ref):
    dh = q_ref.shape[1] // H_MEM

    def head(ref, h):
        x = ref[:, h, :] if len(ref.shape) == 3 else ref[:, h * dh:(h + 1) * dh]
        return x.astype(BF16)

    for h in range(H_MEM):
        sl = slice(h * dh, (h + 1) * dh)
        s = _dot_nt(q_ref[:, sl].astype(BF16), head(mk_ref, h))
        p = jnp.exp(s - jnp.max(s, axis=-1, keepdims=True))
        l = jnp.sum(p, axis=-1, keepdims=True)
        o_ref[:, sl] = (_dot(p.astype(BF16), head(mv_ref, h)) / l).astype(o_ref.dtype)


def _cross_attn(qm, mk, mv, *, rows_per_seq):
    n, d = qm.shape
    n_seq = n // rows_per_seq
    n_mem = mk.shape[0] // n_seq
    tm = min(ROW_TILE, rows_per_seq)
    tps = rows_per_seq // tm
    if mk.ndim == 3:
        mem = pl.BlockSpec((n_mem,) + mk.shape[1:], lambda i: (i // tps, 0, 0))
    else:
        mem = pl.BlockSpec((n_mem, d), lambda i: (i // tps, 0))
    return pl.pallas_call(
        _cross_attn_kernel,
        grid=(n // tm,),
        in_specs=[pl.BlockSpec((tm, d), lambda i: (i, 0)), mem, mem],
        out_specs=pl.BlockSpec((tm, d), lambda i: (i, 0)),
        out_shape=jax.ShapeDtypeStruct((n, d), qm.dtype),
        compiler_params=_params(1),
        name="cross_attn",
    )(qm, mk, mv)


def _mlp_kernel(x_ref, oc_ref, wo_ref, g_ref, wu_ref, wd_ref, gf_ref, y_ref, *, ff_chunk):
    x2 = x_ref[...] + _dot(oc_ref[...].astype(BF16), wo_ref[...])
    xn = _rms(x2, g_ref[...]).astype(BF16)
    acc = x2
    for c in range(wu_ref.shape[1] // ff_chunk):
        sl = slice(c * ff_chunk, (c + 1) * ff_chunk)
        h = jnp.maximum(_dot(xn, wu_ref[:, sl]), 0.0)
        acc = acc + _dot((h * h).astype(BF16), wd_ref[sl, :])
    y_ref[...] = _rms(acc, gf_ref[...])


def _mlp(x1, oc, w_o, g_mlp, w_up, w_down, g_final):
    n, d = x1.shape
    dff = w_up.shape[1]
    tm = min(ROW_TILE, n)
    row = lambda i: (i, 0)
    fixed = lambda i: (0, 0)
    return pl.pallas_call(
        functools.partial(_mlp_kernel, ff_chunk=min(1024, dff)),
        grid=(n // tm,),
        in_specs=[pl.BlockSpec((tm, d), row), pl.BlockSpec((tm, d), row),
                  pl.BlockSpec((d, d), fixed), pl.BlockSpec((1, d), fixed),
                  pl.BlockSpec((d, dff), fixed), pl.BlockSpec((dff, d), fixed),
                  pl.BlockSpec((1, d), fixed)],
        out_specs=pl.BlockSpec((tm, d), row),
        out_shape=jax.ShapeDtypeStruct((n, d), F32),
        compiler_params=_params(1),
        name="mlp",
    )(x1, oc, w_o, g_mlp, w_up, w_down, g_final)


def _rope_angles(pos):
    half = DH // 2
    inv = jnp.exp(jnp.arange(half, dtype=F32) * (-2.0 * math.log(ROPE_THETA) / DH))
    ang = pos.astype(F32)[:, None] * inv[None, :]
    return jnp.cos(ang), jnp.sin(ang)


def _rope_row_tables(pos, reps):
    cos, sin = _rope_angles(pos)
    cos = jnp.tile(jnp.concatenate([cos, cos], axis=-1), (reps, LANES // DH))
    sin = jnp.tile(jnp.concatenate([-sin, sin], axis=-1), (reps, LANES // DH))
    return cos, sin


def kernel(x_prompt, x_sample, mem_prompt, cache_k, cache_v, state_conv, cache_mem_k, cache_mem_v, page_table, g_mix, w_in, lambda_q1, lambda_k1, lambda_q2, lambda_k2, g_subln, conv_w, g_conv, w_out, g_cross, g_mem, w_q_mem, w_k_mem, w_v_mem, w_o_mem, g_mlp, w_up, w_down, g_final):
    assert w_in.shape[0] == 1, "single-layer trunk: the final RMSNorm is fused into the MLP kernel"
    b_p, s_p, d = x_prompt.shape
    b_s, t_s, _ = x_sample.shape
    n_mem = mem_prompt.shape[1]
    dc = conv_w.shape[-1]
    dh_mem = d // H_MEM
    mem_scale = dh_mem ** -0.5
    past_len = page_table.shape[1] * cache_k.shape[2]
    assert t_s >= CONV_W - 1 and (b_s * t_s) % SUBLANES == 0

    pos_p = jnp.arange(s_p)
    cos_pt, sin_pt = (a.T for a in _rope_angles(pos_p))
    tabs_p = _rope_row_tables(pos_p, 1) + (cos_pt, sin_pt)
    tm_s = min(ROW_TILE, b_s * t_s)
    tabs_s = _rope_row_tables(past_len + jnp.arange(t_s), tm_s // t_s)

    xp = x_prompt.reshape(b_p * s_p, d)
    xs = x_sample.reshape(b_s * t_s, d)
    mem = mem_prompt.reshape(b_p * n_mem, d)
    row = lambda a: a.reshape(1, -1)
    l = 0
    lam0 = _lambda_init(l)
    lams = [row(a[l]) for a in (lambda_q1, lambda_k1, lambda_q2, lambda_k2)]
    w_in_b = w_in[l].astype(BF16)
    w_kt_b = w_in[l][:, D_ATTN:2 * D_ATTN].T.astype(BF16)
    w_out_b = w_out[l].astype(BF16)
    w_q_b = w_q_mem[l].astype(BF16)
    w_o_b = w_o_mem[l].astype(BF16)
    w_up_b = w_up[l].astype(BF16)
    w_down_b = w_down[l].astype(BF16)

    def tail(x, o, c, mk, mv, rows_per_seq, act_dtype):
        x1, qm = _mixer_out(x, o, c, w_out_b, row(g_cross[l]), w_q_b, scale=mem_scale,
                            act_dtype=act_dtype)
        oc = _cross_attn(qm, mk, mv, rows_per_seq=rows_per_seq)
        return _mlp(x1, oc, w_o_b, row(g_mlp[l]), w_up_b, w_down_b, row(g_final))

    mk, mv, mkb, mvb = _mem_kv(mem, row(g_mem[l]), w_k_mem[l].astype(BF16),
                               w_v_mem[l].astype(BF16))
    q, kt, ktb, vp, vb, c, conv_p = _mixer_in_prompt(
        xp, row(g_mix[l]), w_in_b, w_kt_b, tabs_p, conv_w[l], row(g_conv[l]), batch=b_p, seq=s_p)
    o = _prompt_attn(q, ktb, vb, lams, row(g_subln[l]), lam0=lam0)
    yp = tail(xp, o, c, mkb, mvb, s_p, BF16)
    kp = kt.reshape(b_p, N_SUB, DH, s_p).transpose(0, 3, 1, 2)

    prev = state_conv[l]
    p1 = jnp.pad(prev[:, 1:2], ((0, 0), (0, t_s - 1), (0, 0))).reshape(b_s * t_s, dc)
    p2 = jnp.pad(prev, ((0, 0), (0, t_s - 2), (0, 0))).reshape(b_s * t_s, dc)
    q, ks, vs, c, u = _mixer_in_sample(xs, row(g_mix[l]), w_in_b, tabs_s, conv_w[l],
                                       row(g_conv[l]), (p1, p2), seq_rows=t_s)
    o = _paged_attn(q, ks, vs, lams, row(g_subln[l]), cache_k[l].transpose(0, 2, 3, 1),
                    cache_v[l], page_table, lam0=lam0)
    ys = tail(xs, o, c, cache_mem_k[l].reshape(b_s * n_mem, H_MEM, dh_mem),
              cache_mem_v[l].reshape(b_s * n_mem, H_MEM, dh_mem), t_s, F32)
    conv_s = u.reshape(b_s, t_s, dc)[:, t_s - (CONV_W - 1):]

    return (yp.reshape(b_p, s_p, d), ys.reshape(b_s, t_s, d),
            kp[None], vp.reshape(1, b_p, s_p, H_A, HEAD_W), conv_p[None],
            mk.reshape(1, b_p, n_mem, H_MEM, dh_mem), mv.reshape(1, b_p, n_mem, H_MEM, dh_mem),
            ks.reshape(1, b_s, t_s, N_SUB, DH), vs.reshape(1, b_s, t_s, H_A, HEAD_W),
            conv_s[None])
```

```python
import functools
import math

import jax
import jax.numpy as jnp
from jax import lax
from jax.experimental import pallas as pl
from jax.experimental.pallas import tpu as pltpu

F32 = jnp.float32
BF16 = jnp.bfloat16

H_A = 4
DH = 64
N_SUB = 2 * H_A
HEAD_W = 2 * DH
D_ATTN = H_A * HEAD_W
CONV_W = 3
H_MEM = 4
ROPE_THETA = 10000.0
EPS = 1e-6
ATTN_SCALE = DH ** -0.5
LANES = 128
SUBLANES = 8
VMEM_LIMIT = 56 * 1024 * 1024

ROW_TILE = 512
PAGES_PER_STEP = 8


def _lambda_init(l):
    return 0.8 - 0.6 * math.exp(-0.3 * l)


def _rms(x, g):
    return x * lax.rsqrt(jnp.mean(x * x, axis=-1, keepdims=True) + EPS) * g


def _dot(a, b):
    return jnp.dot(a, b, preferred_element_type=F32)


def _dot_nt(a, b):
    return lax.dot_general(a, b, (((1,), (1,)), ((), ())), preferred_element_type=F32)


def _params(n_axes):
    return pltpu.CompilerParams(dimension_semantics=("arbitrary",) * n_axes,
                                vmem_limit_bytes=VMEM_LIMIT)


def _diff_lambda(lq1_ref, lk1_ref, lq2_ref, lk2_ref, lam0):
    a = jnp.sum(lq1_ref[...] * lk1_ref[...], axis=-1, keepdims=True)
    b = jnp.sum(lq2_ref[...] * lk2_ref[...], axis=-1, keepdims=True)
    return jnp.exp(a) - jnp.exp(b) + lam0


def _store_heads(ref, x):
    w = ref.shape[2]
    for h in range(ref.shape[1]):
        ref[:, h, :] = x[:, h * w:(h + 1) * w].astype(ref.dtype)


def _mem_kv_kernel(m_ref, g_ref, wk_ref, wv_ref, k_ref, v_ref, kb_ref, vb_ref):
    mn = _rms(m_ref[...], g_ref[...]).astype(BF16)
    k = _dot(mn, wk_ref[...])
    v = _dot(mn, wv_ref[...])
    _store_heads(k_ref, k)
    _store_heads(v_ref, v)
    kb_ref[...] = k.astype(BF16)
    vb_ref[...] = v.astype(BF16)


def _mem_kv(mem, g, wk, wv):
    n, d = mem.shape
    tm = min(ROW_TILE, n)
    row = lambda i: (i, 0)
    fixed = lambda i: (0, 0)
    heads = pl.BlockSpec((tm, H_MEM, d // H_MEM), lambda i: (i, 0, 0))
    heads_shape = jax.ShapeDtypeStruct((n, H_MEM, d // H_MEM), F32)
    return pl.pallas_call(
        _mem_kv_kernel,
        grid=(n // tm,),
        in_specs=[pl.BlockSpec((tm, d), row), pl.BlockSpec((1, d), fixed),
                  pl.BlockSpec((d, d), fixed), pl.BlockSpec((d, d), fixed)],
        out_specs=[heads, heads, pl.BlockSpec((tm, d), row), pl.BlockSpec((tm, d), row)],
        out_shape=[heads_shape, heads_shape] + [jax.ShapeDtypeStruct((n, d), BF16)] * 2,
        compiler_params=_params(1),
        name="mem_kv",
    )(mem, g, wk, wv)


def _rope_rows(p, cos, sin):
    lane = lax.broadcasted_iota(jnp.int32, (p.shape[0], LANES), 1)
    first_half = (lane % DH) < (DH // 2)
    out = []
    for c in range(D_ATTN // LANES):
        pc = p[:, c * LANES:(c + 1) * LANES]
        swapped = jnp.where(first_half, pltpu.roll(pc, LANES - DH // 2, 1),
                            pltpu.roll(pc, DH // 2, 1))
        out.append(pc * cos + swapped * sin)
    return out


def _conv_branch(xn, w_ref, cw_ref, gc_ref, fix_history):
    dc = cw_ref.shape[1]
    o0 = 3 * D_ATTN
    gate_b = _dot(xn, w_ref[:, o0:o0 + dc])
    u = _dot(xn, w_ref[:, o0 + dc:o0 + 2 * dc]) * _dot(xn, w_ref[:, o0 + 2 * dc:o0 + 3 * dc])
    row = lax.broadcasted_iota(jnp.int32, u.shape, 0)
    um1, um2 = fix_history(row, pltpu.roll(u, 1, 0), pltpu.roll(u, 2, 0))
    y = cw_ref[0:1, :] * um2 + cw_ref[1:2, :] * um1 + cw_ref[2:3, :] * u
    return _rms(gate_b * y, gc_ref[...]).astype(BF16), u


def _rope_cols(pt, cos_t, sin_t, outs):
    half = DH // 2
    for s in range(N_SUB):
        x1 = pt[s * DH:s * DH + half, :]
        x2 = pt[s * DH + half:(s + 1) * DH, :]
        r1 = x1 * cos_t - x2 * sin_t
        r2 = x2 * cos_t + x1 * sin_t
        for ref, scale in outs:
            if scale != 1.0:
                r1, r2 = r1 * scale, r2 * scale
            ref[s * DH:s * DH + half, :] = r1.astype(ref.dtype)
            ref[s * DH + half:(s + 1) * DH, :] = r2.astype(ref.dtype)


def _mixer_in_prompt_kernel(x_ref, g_ref, w_ref, wt_ref, cost_ref, sint_ref, cw_ref, gc_ref,
                            qt_ref, kt_ref, kb_ref, v_ref, vt_ref, c_ref, cs_ref, carry_ref, *,
                            tiles_per_seq):
    tm = x_ref.shape[0]
    xn = _rms(x_ref[...], g_ref[...]).astype(BF16)
    cos_t = cost_ref[...]
    sin_t = sint_ref[...]

    _rope_cols(_dot_nt(wt_ref[0:D_ATTN, :], xn), cos_t, sin_t, [(qt_ref, ATTN_SCALE)])
    _rope_cols(_dot_nt(wt_ref[D_ATTN:2 * D_ATTN, :], xn), cos_t, sin_t, [(kt_ref, 1.0)])
    kb_ref[...] = kt_ref[...].T.astype(kb_ref.dtype)
    v = _dot(xn, w_ref[:, 2 * D_ATTN:3 * D_ATTN])
    _store_heads(v_ref, v)
    vt_ref[...] = v.T.astype(vt_ref.dtype)

    @pl.when(pl.program_id(0) % tiles_per_seq == 0)
    def _():
        carry_ref[...] = jnp.zeros_like(carry_ref)

    def fix_history(row, um1, um2):
        prev0 = carry_ref[SUBLANES - 2:SUBLANES - 1, :]
        prev1 = carry_ref[SUBLANES - 1:SUBLANES, :]
        return (jnp.where(row == 0, prev1, um1),
                jnp.where(row == 0, prev0, jnp.where(row == 1, prev1, um2)))

    c, u = _conv_branch(xn, w_ref, cw_ref, gc_ref, fix_history)
    c_ref[...] = c
    carry_ref[...] = u[tm - SUBLANES:tm, :]
    cs_ref[...] = u[tm - (CONV_W - 1):tm, :]


def _mixer_in_prompt(x, g, w_in, w_qk_t, tabs, conv_w, g_conv, *, batch, seq):
    n, d = x.shape
    dc = conv_w.shape[1]
    tm = min(ROW_TILE, seq)
    assert seq % tm == 0
    tps = seq // tm
    cos_t, sin_t = tabs
    row = lambda i: (i, 0)
    fixed = lambda i: (0, 0)
    tile_t = pl.BlockSpec((None, None, D_ATTN, tm), lambda i: (i // tps, i % tps, 0, 0))
    tile_t_shape = jax.ShapeDtypeStruct((batch, tps, D_ATTN, tm), BF16)
    return pl.pallas_call(
        functools.partial(_mixer_in_prompt_kernel, tiles_per_seq=tps),
        grid=(n // tm,),
        in_specs=[pl.BlockSpec((tm, d), row), pl.BlockSpec((1, d), fixed),
                  pl.BlockSpec(w_in.shape, fixed), pl.BlockSpec(w_qk_t.shape, fixed),
                  pl.BlockSpec((DH // 2, tm), lambda i: (0, i % tps)),
                  pl.BlockSpec((DH // 2, tm), lambda i: (0, i % tps)),
                  pl.BlockSpec((CONV_W, dc), fixed), pl.BlockSpec((1, dc), fixed)],
        out_specs=[tile_t,
                   pl.BlockSpec((None, D_ATTN, tm), lambda i: (i // tps, 0, i % tps)),
                   pl.BlockSpec((tm, D_ATTN), row),
                   pl.BlockSpec((tm, H_A, HEAD_W), lambda i: (i, 0, 0)),
                   tile_t,
                   pl.BlockSpec((tm, dc), row),
                   pl.BlockSpec((None, CONV_W - 1, dc), lambda i: (i // tps, 0, 0))],
        out_shape=[tile_t_shape,
                   jax.ShapeDtypeStruct((batch, D_ATTN, seq), F32),
                   jax.ShapeDtypeStruct((n, D_ATTN), BF16),
                   jax.ShapeDtypeStruct((n, H_A, HEAD_W), F32),
                   tile_t_shape,
                   jax.ShapeDtypeStruct((n, dc), BF16),
                   jax.ShapeDtypeStruct((batch, CONV_W - 1, dc), F32)],
        scratch_shapes=[pltpu.VMEM((SUBLANES, dc), F32)],
        compiler_params=_params(1),
        name="mixer_in_prompt",
    )(x, g, w_in, w_qk_t, cos_t, sin_t, conv_w, g_conv)


def _mixer_in_sample_kernel(x_ref, g_ref, w_ref, cos_ref, sin_ref, cw_ref, gc_ref, p1_ref, p2_ref,
                            q_ref, k_ref, v_ref, c_ref, u_ref, *, seq_rows):
    xn = _rms(x_ref[...], g_ref[...]).astype(BF16)
    cos = cos_ref[...]
    sin = sin_ref[...]
    for c, r in enumerate(_rope_rows(_dot(xn, w_ref[:, 0:D_ATTN]), cos, sin)):
        q_ref[:, c * LANES:(c + 1) * LANES] = r * ATTN_SCALE
    for c, r in enumerate(_rope_rows(_dot(xn, w_ref[:, D_ATTN:2 * D_ATTN]), cos, sin)):
        k_ref[:, c * LANES:(c + 1) * LANES] = r
    v_ref[...] = _dot(xn, w_ref[:, 2 * D_ATTN:3 * D_ATTN])

    def fix_history(row, um1, um2):
        t = row % seq_rows
        return jnp.where(t == 0, p1_ref[...], um1), jnp.where(t < 2, p2_ref[...], um2)

    c, u = _conv_branch(xn, w_ref, cw_ref, gc_ref, fix_history)
    c_ref[...] = c
    u_ref[...] = u


def _mixer_in_sample(x, g, w_in, tabs, conv_w, g_conv, hist, *, seq_rows):
    n, d = x.shape
    dc = conv_w.shape[1]
    tm = min(ROW_TILE, n)
    cos, sin = tabs
    row = lambda i: (i, 0)
    fixed = lambda i: (0, 0)
    act = pl.BlockSpec((tm, D_ATTN), row)
    conv = pl.BlockSpec((tm, dc), row)
    return pl.pallas_call(
        functools.partial(_mixer_in_sample_kernel, seq_rows=seq_rows),
        grid=(n // tm,),
        in_specs=[pl.BlockSpec((tm, d), row), pl.BlockSpec((1, d), fixed),
                  pl.BlockSpec(w_in.shape, fixed),
                  pl.BlockSpec((tm, LANES), fixed), pl.BlockSpec((tm, LANES), fixed),
                  pl.BlockSpec((CONV_W, dc), fixed), pl.BlockSpec((1, dc), fixed), conv, conv],
        out_specs=[act, act, act, conv, conv],
        out_shape=[jax.ShapeDtypeStruct((n, D_ATTN), F32)] * 3
                  + [jax.ShapeDtypeStruct((n, dc), BF16), jax.ShapeDtypeStruct((n, dc), F32)],
        compiler_params=_params(1),
        name="mixer_in_sample",
    )(x, g, w_in, cos, sin, conv_w, g_conv, *hist)


def _head_out(a0, l0, a1, l1, lam, g, lam0):
    o = a0 / l0 - lam * (a1 / l1)
    return _rms(o, g) * (1.0 - lam0)


def _prompt_attn_kernel(qt_ref, k_ref, vt_ref, lq1, lk1, lq2, lk2, gs_ref, o_ref,
                        qt2_ref, s_ref, m_ref, l_ref, acc_ref, *, lam0):
    tq = qt_ref.shape[1]
    qi = pl.program_id(2)
    qt = qt_ref[...]
    d_row = lax.broadcasted_iota(jnp.int32, qt.shape, 0)
    qt2_ref[0] = jnp.where(d_row < DH, qt, jnp.zeros_like(qt))
    qt2_ref[1] = jnp.where(d_row >= DH, qt, jnp.zeros_like(qt))
    m_ref[...] = jnp.full_like(m_ref, -jnp.inf)
    l_ref[...] = jnp.zeros_like(l_ref)
    acc_ref[...] = jnp.zeros_like(acc_ref)

    def scores(j, sub):
        k = k_ref[pl.ds(pl.multiple_of(j * tq, tq), tq), :]
        s_ref[sub] = _dot(k, qt2_ref[sub])

    def softmax_pv(j, sub, masked):
        st = s_ref[sub]
        if masked:
            key = lax.broadcasted_iota(jnp.int32, st.shape, 0)
            qry = lax.broadcasted_iota(jnp.int32, st.shape, 1)
            st = jnp.where(key <= qry, st, -jnp.inf)
        m_prev = m_ref[sub]
        m_new = jnp.maximum(m_prev, jnp.max(st, axis=0, keepdims=True))
        p = jnp.exp(st - m_new)
        alpha = jnp.exp(m_prev - m_new)
        l_ref[sub] = alpha * l_ref[sub] + jnp.sum(p, axis=0, keepdims=True)
        acc_ref[sub] = alpha * acc_ref[sub] + _dot(vt_ref[j], p.astype(BF16))
        m_ref[sub] = m_new

    scores(0, 0)

    def body(j, carry):
        scores(j, 1)
        softmax_pv(j, 0, False)
        scores(j + 1, 0)
        softmax_pv(j, 1, False)
        return carry

    lax.fori_loop(0, qi, body, 0)
    scores(qi, 1)
    softmax_pv(qi, 0, True)
    softmax_pv(qi, 1, True)
    lam = _diff_lambda(lq1, lk1, lq2, lk2, lam0)
    ot = acc_ref[0] / l_ref[0] - lam * (acc_ref[1] / l_ref[1])
    o_ref[...] = (_rms(ot.T, gs_ref[...]) * (1.0 - lam0)).astype(o_ref.dtype)


def _prompt_attn(qt, kb, vt, lams, g_subln, *, lam0):
    batch, nk, _, tk = qt.shape
    seq = nk * tk
    vec = pl.BlockSpec((1, DH), lambda b, h, i: (0, 0))
    return pl.pallas_call(
        functools.partial(_prompt_attn_kernel, lam0=lam0),
        grid=(batch, H_A, nk),
        in_specs=[pl.BlockSpec((None, None, HEAD_W, tk), lambda b, h, i: (b, i, h, 0)),
                  pl.BlockSpec((seq, HEAD_W), lambda b, h, i: (b, h)),
                  pl.BlockSpec((None, nk, HEAD_W, tk), lambda b, h, i: (b, 0, h, 0)),
                  vec, vec, vec, vec,
                  pl.BlockSpec((1, HEAD_W), lambda b, h, i: (0, 0))],
        out_specs=pl.BlockSpec((tk, HEAD_W), lambda b, h, i: (b * nk + i, h)),
        out_shape=jax.ShapeDtypeStruct((batch * seq, D_ATTN), BF16),
        scratch_shapes=[pltpu.VMEM((2, HEAD_W, tk), BF16),
                        pltpu.VMEM((2, tk, tk), F32),
                        pltpu.VMEM((2, 1, tk), F32),
                        pltpu.VMEM((2, 1, tk), F32),
                        pltpu.VMEM((2, HEAD_W, tk), F32)],
        compiler_params=_params(3),
        name="prompt_attn",
    )(qt, kb, vt, *lams, g_subln)


def _paged_attn_kernel(pt_ref, q_ref, kn_ref, vn_ref, lq1, lk1, lq2, lk2, gs_ref, *rest,
                       pages, lam0):
    del pt_ref
    k_pages = rest[:pages]
    v_pages = rest[pages:2 * pages]
    o_ref, m_ref, l_ref, acc_ref = rest[2 * pages:]
    t, dq = q_ref.shape
    rows = N_SUB * t
    c = pl.program_id(1)

    q = q_ref[...]
    qt = jnp.concatenate([q] * N_SUB, axis=0)
    r_i = lax.broadcasted_iota(jnp.int32, (rows, dq), 0)
    c_i = lax.broadcasted_iota(jnp.int32, (rows, dq), 1)
    qbd = jnp.where((r_i // t) == (c_i // DH), qt, jnp.zeros_like(qt)).astype(BF16)

    @pl.when(c == 0)
    def _():
        m_ref[...] = jnp.full_like(m_ref, -jnp.inf)
        l_ref[...] = jnp.zeros_like(l_ref)
        acc_ref[...] = jnp.zeros_like(acc_ref)

    def update(scores, values):
        m_prev = m_ref[...]
        m_new = m_prev
        for s in scores:
            m_new = jnp.maximum(m_new, jnp.max(s, axis=-1, keepdims=True))
        alpha = jnp.exp(m_prev - m_new)
        l_new = alpha * l_ref[...]
        acc = alpha * acc_ref[...]
        for s, value in zip(scores, values):
            p = jnp.exp(s - m_new)
            l_new = l_new + jnp.sum(p, axis=-1, keepdims=True)
            pb = p.astype(BF16)
            acc = acc + jnp.concatenate(
                [_dot(pb[2 * h * t:2 * (h + 1) * t, :], value(h)) for h in range(H_A)], axis=0)
        m_ref[...] = m_new
        l_ref[...] = l_new
        acc_ref[...] = acc

    scores = [_dot(qbd, k_pages[j][...].reshape(dq, -1).astype(BF16)) for j in range(pages)]
    page = v_pages[0].shape[0] // H_A
    values = [functools.partial(
        lambda h, j: v_pages[j][pl.ds(h, page, stride=H_A), :].astype(BF16), j=j)
        for j in range(pages)]
    update(scores, values)

    @pl.when(c == pl.num_programs(1) - 1)
    def _():
        pad = jnp.zeros((LANES - t, dq), F32)
        kn = jnp.concatenate([kn_ref[...], pad], axis=0).astype(BF16)
        vn = jnp.concatenate([vn_ref[...], pad], axis=0).astype(BF16)
        s = _dot_nt(qbd, kn)
        key_t = lax.broadcasted_iota(jnp.int32, s.shape, 1)
        qry_t = lax.broadcasted_iota(jnp.int32, s.shape, 0) % t
        update([jnp.where(key_t <= qry_t, s, -jnp.inf)],
               [lambda h: vn[:, h * HEAD_W:(h + 1) * HEAD_W]])
        lam = _diff_lambda(lq1, lk1, lq2, lk2, lam0)
        acc = acc_ref[...]
        l = l_ref[...]
        for h in range(H_A):
            r0 = 2 * h * t
            o = _head_out(acc[r0:r0 + t], l[r0:r0 + t], acc[r0 + t:r0 + 2 * t],
                          l[r0 + t:r0 + 2 * t], lam, gs_ref[...], lam0)
            o_ref[:, h * HEAD_W:(h + 1) * HEAD_W] = o.astype(o_ref.dtype)


def _paged_attn(q, k_new, v_new, lams, g_subln, pool_kt, pool_v, page_table, *, lam0):
    n_seq, n_pages = page_table.shape
    n = q.shape[0]
    t = n // n_seq
    page = pool_kt.shape[3]
    pages = min(PAGES_PER_STEP, n_pages)
    assert n_pages % pages == 0
    seq = pl.BlockSpec((t, D_ATTN), lambda b, c, pt: (b, 0))
    vec = pl.BlockSpec((1, DH), lambda b, c, pt: (0, 0))

    def page_index(j, ndim):
        return lambda b, c, pt: (pt[b * n_pages + c * pages + j],) + (0,) * (ndim - 1)

    rows = N_SUB * t
    grid_spec = pltpu.PrefetchScalarGridSpec(
        num_scalar_prefetch=1,
        grid=(n_seq, n_pages // pages),
        in_specs=[seq, seq, seq, vec, vec, vec, vec,
                  pl.BlockSpec((1, HEAD_W), lambda b, c, pt: (0, 0))]
                 + [pl.BlockSpec((None, N_SUB, DH, page), page_index(j, 4)) for j in range(pages)]
                 + [pl.BlockSpec((None, page * H_A, HEAD_W), page_index(j, 3))
                    for j in range(pages)],
        out_specs=seq,
        scratch_shapes=[pltpu.VMEM((rows, 1), F32), pltpu.VMEM((rows, 1), F32),
                        pltpu.VMEM((rows, HEAD_W), F32)],
    )
    return pl.pallas_call(
        functools.partial(_paged_attn_kernel, pages=pages, lam0=lam0),
        grid_spec=grid_spec,
        out_shape=jax.ShapeDtypeStruct((n, D_ATTN), F32),
        compiler_params=_params(2),
        name="paged_attn",
    )(page_table.reshape(-1), q, k_new, v_new, *lams, g_subln,
      *([pool_kt] * pages), *([pool_v] * pages))


def _mixer_out_kernel(x_ref, o_ref, c_ref, wo_ref, g_ref, wq_ref, x1_ref, qm_ref, *, scale):
    da = o_ref.shape[1]
    x1 = (x_ref[...] + _dot(o_ref[...].astype(BF16), wo_ref[0:da, :])
          + _dot(c_ref[...], wo_ref[da:, :]))
    x1_ref[...] = x1
    xn = _rms(x1, g_ref[...]).astype(BF16)
    qm_ref[...] = (_dot(xn, wq_ref[...]) * scale).astype(qm_ref.dtype)


def _mixer_out(x, o, c, w_out, g_cross, w_q, *, scale, act_dtype):
    n, d = x.shape
    tm = min(ROW_TILE, n)
    row = lambda i: (i, 0)
    fixed = lambda i: (0, 0)
    return pl.pallas_call(
        functools.partial(_mixer_out_kernel, scale=scale),
        grid=(n // tm,),
        in_specs=[pl.BlockSpec((tm, d), row), pl.BlockSpec((tm, o.shape[1]), row),
                  pl.BlockSpec((tm, c.shape[1]), row), pl.BlockSpec((d, d), fixed),
                  pl.BlockSpec((1, d), fixed), pl.BlockSpec((d, d), fixed)],
        out_specs=[pl.BlockSpec((tm, d), row)] * 2,
        out_shape=[jax.ShapeDtypeStruct((n, d), F32), jax.ShapeDtypeStruct((n, d), act_dtype)],
        compiler_params=_params(1),
        name="mixer_out",
    )(x, o, c, w_out, g_cross, w_q)


def _cross_attn_kernel(q_ref, mk_ref, mv_ref, o_ref):
    dh = q_ref.shape[1] // H_MEM

    def head(ref, h):
        if ref.shape[1] == q_ref.shape[1]:
            return ref[:, h * dh:(h + 1) * dh].astype(BF16)
        chunks = dh // LANES
        step = chunks * H_MEM
        n_mem = ref.shape[0] // step
        return jnp.concatenate([ref[pl.ds(c * H_MEM + h, n_mem, stride=step), :]
                                for c in range(chunks)], axis=1).astype(BF16)

    for h in range(H_MEM):
        sl = slice(h * dh, (h + 1) * dh)
        s = _dot_nt(q_ref[:, sl].astype(BF16), head(mk_ref, h))
        p = jnp.exp(s - jnp.max(s, axis=-1, keepdims=True))
        l = jnp.sum(p, axis=-1, keepdims=True)
        o_ref[:, sl] = (_dot(p.astype(BF16), head(mv_ref, h)) / l).astype(o_ref.dtype)


def _cross_attn(qm, mk, mv, *, rows_per_seq):
    n, d = qm.shape
    n_seq = n // rows_per_seq
    tm = min(ROW_TILE, rows_per_seq)
    tps = rows_per_seq // tm
    mem = pl.BlockSpec((mk.shape[0] // n_seq, mk.shape[1]), lambda i: (i // tps, 0))
    return pl.pallas_call(
        _cross_attn_kernel,
        grid=(n // tm,),
        in_specs=[pl.BlockSpec((tm, d), lambda i: (i, 0)), mem, mem],
        out_specs=pl.BlockSpec((tm, d), lambda i: (i, 0)),
        out_shape=jax.ShapeDtypeStruct((n, d), qm.dtype),
        compiler_params=_params(1),
        name="cross_attn",
    )(qm, mk, mv)


def _mlp_kernel(x_ref, oc_ref, wo_ref, g_ref, wu_ref, wd_ref, gf_ref, y_ref, *, ff_chunk):
    x2 = x_ref[...] + _dot(oc_ref[...].astype(BF16), wo_ref[...])
    xn = _rms(x2, g_ref[...]).astype(BF16)
    acc = x2
    for c in range(wu_ref.shape[1] // ff_chunk):
        sl = slice(c * ff_chunk, (c + 1) * ff_chunk)
        h = jnp.maximum(_dot(xn, wu_ref[:, sl]), 0.0)
        acc = acc + _dot((h * h).astype(BF16), wd_ref[sl, :])
    y_ref[...] = _rms(acc, gf_ref[...])


def _mlp(x1, oc, w_o, g_mlp, w_up, w_down, g_final):
    n, d = x1.shape
    dff = w_up.shape[1]
    tm = min(ROW_TILE, n)
    row = lambda i: (i, 0)
    fixed = lambda i: (0, 0)
    return pl.pallas_call(
        functools.partial(_mlp_kernel, ff_chunk=min(1024, dff)),
        grid=(n // tm,),
        in_specs=[pl.BlockSpec((tm, d), row), pl.BlockSpec((tm, d), row),
                  pl.BlockSpec((d, d), fixed), pl.BlockSpec((1, d), fixed),
                  pl.BlockSpec((d, dff), fixed), pl.BlockSpec((dff, d), fixed),
                  pl.BlockSpec((1, d), fixed)],
        out_specs=pl.BlockSpec((tm, d), row),
        out_shape=jax.ShapeDtypeStruct((n, d), F32),
        compiler_params=_params(1),
        name="mlp",
    )(x1, oc, w_o, g_mlp, w_up, w_down, g_final)


def _rope_angles(pos):
    half = DH // 2
    inv = jnp.exp(jnp.arange(half, dtype=F32) * (-2.0 * math.log(ROPE_THETA) / DH))
    ang = pos.astype(F32)[:, None] * inv[None, :]
    return jnp.cos(ang), jnp.sin(ang)


def _rope_row_tables(pos, reps):
    cos, sin = _rope_angles(pos)
    cos = jnp.tile(jnp.concatenate([cos, cos], axis=-1), (reps, LANES // DH))
    sin = jnp.tile(jnp.concatenate([-sin, sin], axis=-1), (reps, LANES // DH))
    return cos, sin


def kernel(x_prompt, x_sample, mem_prompt, cache_k, cache_v, state_conv, cache_mem_k, cache_mem_v, page_table, g_mix, w_in, lambda_q1, lambda_k1, lambda_q2, lambda_k2, g_subln, conv_w, g_conv, w_out, g_cross, g_mem, w_q_mem, w_k_mem, w_v_mem, w_o_mem, g_mlp, w_up, w_down, g_final):
    assert w_in.shape[0] == 1, "single-layer trunk: the final RMSNorm is fused into the MLP kernel"
    b_p, s_p, d = x_prompt.shape
    b_s, t_s, _ = x_sample.shape
    n_mem = mem_prompt.shape[1]
    dc = conv_w.shape[-1]
    dh_mem = d // H_MEM
    mem_scale = dh_mem ** -0.5
    past_len = page_table.shape[1] * cache_k.shape[2]
    assert t_s >= CONV_W - 1 and (b_s * t_s) % SUBLANES == 0

    pos_p = jnp.arange(s_p)
    tabs_p = tuple(a.T for a in _rope_angles(pos_p))
    tm_s = min(ROW_TILE, b_s * t_s)
    tabs_s = _rope_row_tables(past_len + jnp.arange(t_s), tm_s // t_s)

    xp = x_prompt.reshape(b_p * s_p, d)
    xs = x_sample.reshape(b_s * t_s, d)
    mem = mem_prompt.reshape(b_p * n_mem, d)
    row = lambda a: a.reshape(1, -1)
    l = 0
    lam0 = _lambda_init(l)
    lams = [row(a[l]) for a in (lambda_q1, lambda_k1, lambda_q2, lambda_k2)]
    w_in_b = w_in[l].astype(BF16)
    w_qk_t = w_in[l][:, 0:2 * D_ATTN].T.astype(BF16)
    w_out_b = w_out[l].astype(BF16)
    w_q_b = w_q_mem[l].astype(BF16)
    w_o_b = w_o_mem[l].astype(BF16)
    w_up_b = w_up[l].astype(BF16)
    w_down_b = w_down[l].astype(BF16)

    def tail(x, o, c, mk, mv, rows_per_seq, act_dtype):
        x1, qm = _mixer_out(x, o, c, w_out_b, row(g_cross[l]), w_q_b, scale=mem_scale,
                            act_dtype=act_dtype)
        oc = _cross_attn(qm, mk, mv, rows_per_seq=rows_per_seq)
        return _mlp(x1, oc, w_o_b, row(g_mlp[l]), w_up_b, w_down_b, row(g_final))

    mk, mv, mkb, mvb = _mem_kv(mem, row(g_mem[l]), w_k_mem[l].astype(BF16),
                               w_v_mem[l].astype(BF16))
    qt, kt, kb, vp, vt, c, conv_p = _mixer_in_prompt(
        xp, row(g_mix[l]), w_in_b, w_qk_t, tabs_p, conv_w[l], row(g_conv[l]), batch=b_p, seq=s_p)
    o = _prompt_attn(qt, kb, vt, lams, row(g_subln[l]), lam0=lam0)
    yp = tail(xp, o, c, mkb, mvb, s_p, BF16)
    kp = kt.reshape(b_p, N_SUB, DH, s_p).transpose(0, 3, 1, 2)

    prev = state_conv[l]
    p1 = jnp.pad(prev[:, 1:2], ((0, 0), (0, t_s - 1), (0, 0))).reshape(b_s * t_s, dc)
    p2 = jnp.pad(prev, ((0, 0), (0, t_s - 2), (0, 0))).reshape(b_s * t_s, dc)
    q, ks, vs, c, u = _mixer_in_sample(xs, row(g_mix[l]), w_in_b, tabs_s, conv_w[l],
                                       row(g_conv[l]), (p1, p2), seq_rows=t_s)
    o = _paged_attn(q, ks, vs, lams, row(g_subln[l]), cache_k[l].transpose(0, 2, 3, 1),
                    cache_v[l].reshape(cache_v.shape[1], -1, HEAD_W), page_table, lam0=lam0)

    def lane_rows(m):
        m = m.reshape(b_s * n_mem, H_MEM, dh_mem // LANES, LANES)
        return m.transpose(0, 2, 1, 3).reshape(-1, LANES)

    ys = tail(xs, o, c, lane_rows(cache_mem_k[l]), lane_rows(cache_mem_v[l]), t_s, F32)
    conv_s = u.reshape(b_s, t_s, dc)[:, t_s - (CONV_W - 1):]

    return (yp.reshape(b_p, s_p, d), ys.reshape(b_s, t_s, d),
            kp[None], vp.reshape(1, b_p, s_p, H_A, HEAD_W), conv_p[None],
            mk.reshape(1, b_p, n_mem, H_MEM, dh_mem), mv.reshape(1, b_p, n_mem, H_MEM, dh_mem),
            ks.reshape(1, b_s, t_s, N_SUB, DH), vs.reshape(1, b_s, t_s, H_A, HEAD_W),
            conv_s[None])
```

```python
import functools
import math

import jax
import jax.numpy as jnp
from jax import lax
from jax.experimental import pallas as pl
from jax.experimental.pallas import tpu as pltpu

F32 = jnp.float32
BF16 = jnp.bfloat16

H_A = 4
DH = 64
N_SUB = 2 * H_A
HEAD_W = 2 * DH
D_ATTN = H_A * HEAD_W
CONV_W = 3
H_MEM = 4
ROPE_THETA = 10000.0
EPS = 1e-6
ATTN_SCALE = DH ** -0.5
LANES = 128
SUBLANES = 8
VMEM_LIMIT = 56 * 1024 * 1024

ONES_ROWS = 16
ROW_TILE = 512
SHORT_SEQS_PER_STEP = 4
PAGES_PER_STEP = 8


def _lambda_init(l):
    return 0.8 - 0.6 * math.exp(-0.3 * l)


def _rms(x, g):
    return x * lax.rsqrt(jnp.mean(x * x, axis=-1, keepdims=True) + EPS) * g


def _dot(a, b):
    return jnp.dot(a, b, preferred_element_type=F32)


def _dot_nt(a, b):
    return lax.dot_general(a, b, (((1,), (1,)), ((), ())), preferred_element_type=F32)


def _params(n_axes):
    return pltpu.CompilerParams(dimension_semantics=("arbitrary",) * n_axes,
                                vmem_limit_bytes=VMEM_LIMIT)


def _diff_lambda(lq1_ref, lk1_ref, lq2_ref, lk2_ref, lam0):
    a = jnp.sum(lq1_ref[...] * lk1_ref[...], axis=-1, keepdims=True)
    b = jnp.sum(lq2_ref[...] * lk2_ref[...], axis=-1, keepdims=True)
    return jnp.exp(a) - jnp.exp(b) + lam0


def _store_heads(ref, x):
    w = ref.shape[2]
    for h in range(ref.shape[1]):
        ref[:, h, :] = x[:, h * w:(h + 1) * w].astype(ref.dtype)


def _mem_kv_kernel(m_ref, g_ref, wk_ref, wv_ref, k_ref, v_ref, kb_ref, vb_ref):
    mn = _rms(m_ref[...], g_ref[...]).astype(BF16)
    k = _dot(mn, wk_ref[...])
    v = _dot(mn, wv_ref[...])
    _store_heads(k_ref, k)
    _store_heads(v_ref, v)
    kb_ref[...] = k.astype(BF16)
    vb_ref[...] = v.astype(BF16)


def _mem_kv(mem, g, wk, wv):
    n, d = mem.shape
    tm = min(ROW_TILE, n)
    row = lambda i: (i, 0)
    fixed = lambda i: (0, 0)
    heads = pl.BlockSpec((tm, H_MEM, d // H_MEM), lambda i: (i, 0, 0))
    heads_shape = jax.ShapeDtypeStruct((n, H_MEM, d // H_MEM), F32)
    return pl.pallas_call(
        _mem_kv_kernel,
        grid=(n // tm,),
        in_specs=[pl.BlockSpec((tm, d), row), pl.BlockSpec((1, d), fixed),
                  pl.BlockSpec((d, d), fixed), pl.BlockSpec((d, d), fixed)],
        out_specs=[heads, heads, pl.BlockSpec((tm, d), row), pl.BlockSpec((tm, d), row)],
        out_shape=[heads_shape, heads_shape] + [jax.ShapeDtypeStruct((n, d), BF16)] * 2,
        compiler_params=_params(1),
        name="mem_kv",
    )(mem, g, wk, wv)


def _rope_rows(p, cos, sin):
    lane = lax.broadcasted_iota(jnp.int32, (p.shape[0], LANES), 1)
    first_half = (lane % DH) < (DH // 2)
    out = []
    for c in range(D_ATTN // LANES):
        pc = p[:, c * LANES:(c + 1) * LANES]
        swapped = jnp.where(first_half, pltpu.roll(pc, LANES - DH // 2, 1),
                            pltpu.roll(pc, DH // 2, 1))
        out.append(pc * cos + swapped * sin)
    return out


def _conv_branch(xn, w_ref, cw_ref, gc_ref, fix_history):
    dc = cw_ref.shape[1]
    o0 = 3 * D_ATTN
    gate_b = _dot(xn, w_ref[:, o0:o0 + dc])
    u = _dot(xn, w_ref[:, o0 + dc:o0 + 2 * dc]) * _dot(xn, w_ref[:, o0 + 2 * dc:o0 + 3 * dc])
    row = lax.broadcasted_iota(jnp.int32, u.shape, 0)
    um1, um2 = fix_history(row, pltpu.roll(u, 1, 0), pltpu.roll(u, 2, 0))
    y = cw_ref[0:1, :] * um2 + cw_ref[1:2, :] * um1 + cw_ref[2:3, :] * u
    return _rms(gate_b * y, gc_ref[...]).astype(BF16), u


def _rope_cols(pt, cos_t, sin_t, outs):
    half = DH // 2
    for s in range(N_SUB):
        x1 = pt[s * DH:s * DH + half, :]
        x2 = pt[s * DH + half:(s + 1) * DH, :]
        r1 = x1 * cos_t - x2 * sin_t
        r2 = x2 * cos_t + x1 * sin_t
        for ref, scale in outs:
            if scale != 1.0:
                r1, r2 = r1 * scale, r2 * scale
            ref[s * DH:s * DH + half, :] = r1.astype(ref.dtype)
            ref[s * DH + half:(s + 1) * DH, :] = r2.astype(ref.dtype)


def _mixer_in_prompt_kernel(x_ref, g_ref, w_ref, wt_ref, cost_ref, sint_ref, cw_ref, gc_ref,
                            qt_ref, kt_ref, kb_ref, v_ref, vt_ref, c_ref, cs_ref, carry_ref, *,
                            tiles_per_seq):
    tm = x_ref.shape[0]
    xn = _rms(x_ref[...], g_ref[...]).astype(BF16)
    cos_t = cost_ref[...]
    sin_t = sint_ref[...]

    _rope_cols(_dot_nt(wt_ref[0:D_ATTN, :], xn), cos_t, sin_t, [(qt_ref, ATTN_SCALE)])
    _rope_cols(_dot_nt(wt_ref[D_ATTN:2 * D_ATTN, :], xn), cos_t, sin_t, [(kt_ref, 1.0)])
    kb_ref[...] = kt_ref[...].T.astype(kb_ref.dtype)
    v = _dot(xn, w_ref[:, 2 * D_ATTN:3 * D_ATTN])
    _store_heads(v_ref, v)
    vt_ref[...] = v.T.astype(vt_ref.dtype)

    @pl.when(pl.program_id(0) % tiles_per_seq == 0)
    def _():
        carry_ref[...] = jnp.zeros_like(carry_ref)

    def fix_history(row, um1, um2):
        prev0 = carry_ref[SUBLANES - 2:SUBLANES - 1, :]
        prev1 = carry_ref[SUBLANES - 1:SUBLANES, :]
        return (jnp.where(row == 0, prev1, um1),
                jnp.where(row == 0, prev0, jnp.where(row == 1, prev1, um2)))

    c, u = _conv_branch(xn, w_ref, cw_ref, gc_ref, fix_history)
    c_ref[...] = c
    carry_ref[...] = u[tm - SUBLANES:tm, :]
    cs_ref[...] = u[tm - (CONV_W - 1):tm, :]


def _mixer_in_prompt(x, g, w_in, w_qk_t, tabs, conv_w, g_conv, *, batch, seq):
    n, d = x.shape
    dc = conv_w.shape[1]
    tm = min(ROW_TILE, seq)
    assert seq % tm == 0
    tps = seq // tm
    cos_t, sin_t = tabs
    row = lambda i: (i, 0)
    fixed = lambda i: (0, 0)
    tile_t = pl.BlockSpec((None, None, D_ATTN, tm), lambda i: (i // tps, i % tps, 0, 0))
    tile_t_shape = jax.ShapeDtypeStruct((batch, tps, D_ATTN, tm), BF16)
    return pl.pallas_call(
        functools.partial(_mixer_in_prompt_kernel, tiles_per_seq=tps),
        grid=(n // tm,),
        in_specs=[pl.BlockSpec((tm, d), row), pl.BlockSpec((1, d), fixed),
                  pl.BlockSpec(w_in.shape, fixed), pl.BlockSpec(w_qk_t.shape, fixed),
                  pl.BlockSpec((DH // 2, tm), lambda i: (0, i % tps)),
                  pl.BlockSpec((DH // 2, tm), lambda i: (0, i % tps)),
                  pl.BlockSpec((CONV_W, dc), fixed), pl.BlockSpec((1, dc), fixed)],
        out_specs=[tile_t,
                   pl.BlockSpec((None, D_ATTN, tm), lambda i: (i // tps, 0, i % tps)),
                   pl.BlockSpec((tm, D_ATTN), row),
                   pl.BlockSpec((tm, H_A, HEAD_W), lambda i: (i, 0, 0)),
                   tile_t,
                   pl.BlockSpec((tm, dc), row),
                   pl.BlockSpec((None, CONV_W - 1, dc), lambda i: (i // tps, 0, 0))],
        out_shape=[tile_t_shape,
                   jax.ShapeDtypeStruct((batch, D_ATTN, seq), F32),
                   jax.ShapeDtypeStruct((n, D_ATTN), BF16),
                   jax.ShapeDtypeStruct((n, H_A, HEAD_W), F32),
                   tile_t_shape,
                   jax.ShapeDtypeStruct((n, dc), BF16),
                   jax.ShapeDtypeStruct((batch, CONV_W - 1, dc), F32)],
        scratch_shapes=[pltpu.VMEM((SUBLANES, dc), F32)],
        compiler_params=_params(1),
        name="mixer_in_prompt",
    )(x, g, w_in, w_qk_t, cos_t, sin_t, conv_w, g_conv)


def _mixer_in_sample_kernel(x_ref, g_ref, w_ref, cos_ref, sin_ref, cw_ref, gc_ref, p1_ref, p2_ref,
                            q_ref, k_ref, v_ref, c_ref, u_ref, *, seq_rows):
    xn = _rms(x_ref[...], g_ref[...]).astype(BF16)
    cos = cos_ref[...]
    sin = sin_ref[...]
    for c, r in enumerate(_rope_rows(_dot(xn, w_ref[:, 0:D_ATTN]), cos, sin)):
        q_ref[:, c * LANES:(c + 1) * LANES] = r * ATTN_SCALE
    for c, r in enumerate(_rope_rows(_dot(xn, w_ref[:, D_ATTN:2 * D_ATTN]), cos, sin)):
        k_ref[:, c * LANES:(c + 1) * LANES] = r
    v_ref[...] = _dot(xn, w_ref[:, 2 * D_ATTN:3 * D_ATTN])

    def fix_history(row, um1, um2):
        t = row % seq_rows
        return jnp.where(t == 0, p1_ref[...], um1), jnp.where(t < 2, p2_ref[...], um2)

    c, u = _conv_branch(xn, w_ref, cw_ref, gc_ref, fix_history)
    c_ref[...] = c
    u_ref[...] = u


def _mixer_in_sample(x, g, w_in, tabs, conv_w, g_conv, hist, *, seq_rows):
    n, d = x.shape
    dc = conv_w.shape[1]
    tm = min(ROW_TILE, n)
    cos, sin = tabs
    row = lambda i: (i, 0)
    fixed = lambda i: (0, 0)
    act = pl.BlockSpec((tm, D_ATTN), row)
    conv = pl.BlockSpec((tm, dc), row)
    return pl.pallas_call(
        functools.partial(_mixer_in_sample_kernel, seq_rows=seq_rows),
        grid=(n // tm,),
        in_specs=[pl.BlockSpec((tm, d), row), pl.BlockSpec((1, d), fixed),
                  pl.BlockSpec(w_in.shape, fixed),
                  pl.BlockSpec((tm, LANES), fixed), pl.BlockSpec((tm, LANES), fixed),
                  pl.BlockSpec((CONV_W, dc), fixed), pl.BlockSpec((1, dc), fixed), conv, conv],
        out_specs=[act, act, act, conv, conv],
        out_shape=[jax.ShapeDtypeStruct((n, D_ATTN), F32)] * 3
                  + [jax.ShapeDtypeStruct((n, dc), BF16), jax.ShapeDtypeStruct((n, dc), F32)],
        compiler_params=_params(1),
        name="mixer_in_sample",
    )(x, g, w_in, cos, sin, conv_w, g_conv, *hist)


def _head_out(a0, l0, a1, l1, lam, g, lam0):
    o = a0 / l0 - lam * (a1 / l1)
    return _rms(o, g) * (1.0 - lam0)


def _prompt_attn_kernel(qt_ref, k_ref, vt_ref, lq1, lk1, lq2, lk2, gs_ref, o_ref,
                        qt2_ref, s_ref, vx_ref, m_ref, acc_ref, *, lam0):
    tq = qt_ref.shape[1]
    qi = pl.program_id(2)
    qt = qt_ref[...]
    d_row = lax.broadcasted_iota(jnp.int32, qt.shape, 0)
    qt2_ref[0] = jnp.where(d_row < DH, qt, jnp.zeros_like(qt))
    qt2_ref[1] = jnp.where(d_row >= DH, qt, jnp.zeros_like(qt))
    m_ref[...] = jnp.full_like(m_ref, -jnp.inf)
    acc_ref[...] = jnp.zeros_like(acc_ref)

    @pl.when(qi == 0)
    def _():
        vx_ref[:, 0:HEAD_W, :] = vt_ref[...]
        vx_ref[:, HEAD_W:, :] = jnp.ones((vx_ref.shape[0], ONES_ROWS, tq), vx_ref.dtype)

    def scores(j, sub):
        k = k_ref[pl.ds(pl.multiple_of(j * tq, tq), tq), :]
        s_ref[sub] = _dot(k, qt2_ref[sub])

    def softmax_pv(j, sub, masked):
        st = s_ref[sub]
        if masked:
            key = lax.broadcasted_iota(jnp.int32, st.shape, 0)
            qry = lax.broadcasted_iota(jnp.int32, st.shape, 1)
            st = jnp.where(key <= qry, st, -jnp.inf)
        m_prev = m_ref[sub]
        m_new = jnp.maximum(m_prev, jnp.max(st, axis=0, keepdims=True))
        p = jnp.exp(st - m_new)
        alpha = jnp.exp(m_prev - m_new)
        acc_ref[sub] = alpha * acc_ref[sub] + _dot(vx_ref[j], p.astype(BF16))
        m_ref[sub] = m_new

    scores(0, 0)

    def body(j, carry):
        scores(j, 1)
        softmax_pv(j, 0, False)
        scores(j + 1, 0)
        softmax_pv(j, 1, False)
        return carry

    lax.fori_loop(0, qi, body, 0)
    scores(qi, 1)
    softmax_pv(qi, 0, True)
    softmax_pv(qi, 1, True)
    lam = _diff_lambda(lq1, lk1, lq2, lk2, lam0)
    a0 = acc_ref[0]
    a1 = acc_ref[1]
    ot = (a0[0:HEAD_W] / a0[HEAD_W:HEAD_W + 1]
          - lam * (a1[0:HEAD_W] / a1[HEAD_W:HEAD_W + 1]))
    o_ref[...] = (_rms(ot.T, gs_ref[...]) * (1.0 - lam0)).astype(o_ref.dtype)


def _prompt_attn(qt, kb, vt, lams, g_subln, *, lam0):
    batch, nk, _, tk = qt.shape
    seq = nk * tk
    vec = pl.BlockSpec((1, DH), lambda b, h, i: (0, 0))
    return pl.pallas_call(
        functools.partial(_prompt_attn_kernel, lam0=lam0),
        grid=(batch, H_A, nk),
        in_specs=[pl.BlockSpec((None, None, HEAD_W, tk), lambda b, h, i: (b, i, h, 0)),
                  pl.BlockSpec((seq, HEAD_W), lambda b, h, i: (b, h)),
                  pl.BlockSpec((None, nk, HEAD_W, tk), lambda b, h, i: (b, 0, h, 0)),
                  vec, vec, vec, vec,
                  pl.BlockSpec((1, HEAD_W), lambda b, h, i: (0, 0))],
        out_specs=pl.BlockSpec((tk, HEAD_W), lambda b, h, i: (b * nk + i, h)),
        out_shape=jax.ShapeDtypeStruct((batch * seq, D_ATTN), BF16),
        scratch_shapes=[pltpu.VMEM((2, HEAD_W, tk), BF16),
                        pltpu.VMEM((2, tk, tk), F32),
                        pltpu.VMEM((nk, HEAD_W + ONES_ROWS, tk), BF16),
                        pltpu.VMEM((2, 1, tk), F32),
                        pltpu.VMEM((2, HEAD_W + ONES_ROWS, tk), F32)],
        compiler_params=_params(3),
        name="prompt_attn",
    )(qt, kb, vt, *lams, g_subln)


def _paged_attn_kernel(pt_ref, q_ref, kn_ref, vn_ref, lq1, lk1, lq2, lk2, gs_ref, kpool, vpool,
                       o_ref, kbuf, vbuf, sem, qbd_ref, m_ref, l_ref, acc_ref, *,
                       pages, chunks_per_seq, t, lam0):
    dq = q_ref.shape[1]
    rows = N_SUB * t
    n_chunks = q_ref.shape[0] // t * chunks_per_seq
    page = kbuf.shape[-1]

    def page_copies(g, slot):
        out = []
        for j in range(pages):
            pid = pt_ref[g * pages + j]
            out.append(pltpu.make_async_copy(kpool.at[pid], kbuf.at[slot, j], sem.at[0, slot, j]))
            out.append(pltpu.make_async_copy(vpool.at[pid], vbuf.at[slot, j], sem.at[1, slot, j]))
        return out

    def start(g, slot):
        for cp in page_copies(g, slot):
            cp.start()

    def wait(g, slot):
        for cp in page_copies(g, slot):
            cp.wait()

    def update(s, value):
        m_prev = m_ref[...]
        m_new = jnp.maximum(m_prev, jnp.max(s, axis=-1, keepdims=True))
        alpha = jnp.exp(m_prev - m_new)
        p = jnp.exp(s - m_new)
        l_ref[...] = alpha * l_ref[...] + jnp.sum(p, axis=-1, keepdims=True)
        pb = p.astype(BF16)
        acc_ref[...] = alpha * acc_ref[...] + jnp.concatenate(
            [_dot(pb[2 * h * t:2 * (h + 1) * t, :], value(h)) for h in range(H_A)], axis=0)
        m_ref[...] = m_new

    def compute(g, slot):
        c = g % chunks_per_seq
        seq_rows = pl.ds(pl.multiple_of((g // chunks_per_seq) * t, t), t)

        @pl.when(c == 0)
        def _():
            qt = jnp.concatenate([q_ref[seq_rows, :]] * N_SUB, axis=0)
            r_i = lax.broadcasted_iota(jnp.int32, (rows, dq), 0)
            c_i = lax.broadcasted_iota(jnp.int32, (rows, dq), 1)
            qbd_ref[...] = jnp.where((r_i // t) == (c_i // DH), qt,
                                     jnp.zeros_like(qt)).astype(BF16)
            m_ref[...] = jnp.full_like(m_ref, -jnp.inf)
            l_ref[...] = jnp.zeros_like(l_ref)
            acc_ref[...] = jnp.zeros_like(acc_ref)

        kt = jnp.concatenate([kbuf[slot, j].reshape(dq, page).astype(BF16)
                              for j in range(pages)], axis=1)
        update(_dot(qbd_ref[...], kt),
               lambda h: jnp.concatenate(
                   [vbuf[slot, j, pl.ds(h, page, stride=H_A), :].astype(BF16)
                    for j in range(pages)], axis=0))

        @pl.when(c == chunks_per_seq - 1)
        def _():
            pad = jnp.zeros((LANES - t, dq), F32)
            kn = jnp.concatenate([kn_ref[seq_rows, :], pad], axis=0).astype(BF16)
            vn = jnp.concatenate([vn_ref[seq_rows, :], pad], axis=0).astype(BF16)
            s = _dot_nt(qbd_ref[...], kn)
            key_t = lax.broadcasted_iota(jnp.int32, s.shape, 1)
            qry_t = lax.broadcasted_iota(jnp.int32, s.shape, 0) % t
            update(jnp.where(key_t <= qry_t, s, -jnp.inf),
                   lambda h: vn[:, h * HEAD_W:(h + 1) * HEAD_W])
            lam = _diff_lambda(lq1, lk1, lq2, lk2, lam0)
            acc = acc_ref[...]
            l = l_ref[...]
            o = [_head_out(acc[2 * h * t:(2 * h + 1) * t], l[2 * h * t:(2 * h + 1) * t],
                           acc[(2 * h + 1) * t:(2 * h + 2) * t], l[(2 * h + 1) * t:(2 * h + 2) * t],
                           lam, gs_ref[...], lam0) for h in range(H_A)]
            o_ref[seq_rows, :] = jnp.concatenate(o, axis=1).astype(o_ref.dtype)

    start(0, 0)

    def body(i, carry):
        g = 2 * i
        start(g + 1, 1)
        wait(g, 0)
        compute(g, 0)

        @pl.when(g + 2 < n_chunks)
        def _():
            start(g + 2, 0)

        wait(g + 1, 1)
        compute(g + 1, 1)
        return carry

    lax.fori_loop(0, n_chunks // 2, body, 0)


def _paged_attn(q, k_new, v_new, lams, g_subln, pool_kt, pool_v, page_table, *, lam0):
    n_seq, n_pages = page_table.shape
    n = q.shape[0]
    t = n // n_seq
    page = pool_kt.shape[3]
    pages = min(PAGES_PER_STEP, n_pages)
    assert n_pages % pages == 0 and (n_seq * n_pages // pages) % 2 == 0
    full = lambda shape: pl.BlockSpec(shape, lambda i, pt: (0,) * len(shape))
    act = full((n, D_ATTN))
    vec = full((1, DH))
    rows = N_SUB * t
    grid_spec = pltpu.PrefetchScalarGridSpec(
        num_scalar_prefetch=1,
        grid=(1,),
        in_specs=[act, act, act, vec, vec, vec, vec, full((1, HEAD_W)),
                  pl.BlockSpec(memory_space=pl.ANY), pl.BlockSpec(memory_space=pl.ANY)],
        out_specs=act,
        scratch_shapes=[pltpu.VMEM((2, pages, N_SUB, DH, page), F32),
                        pltpu.VMEM((2, pages, page * H_A, HEAD_W), F32),
                        pltpu.SemaphoreType.DMA((2, 2, pages)),
                        pltpu.VMEM((rows, D_ATTN), BF16),
                        pltpu.VMEM((rows, 1), F32), pltpu.VMEM((rows, 1), F32),
                        pltpu.VMEM((rows, HEAD_W), F32)],
    )
    return pl.pallas_call(
        functools.partial(_paged_attn_kernel, pages=pages, chunks_per_seq=n_pages // pages, t=t,
                          lam0=lam0),
        grid_spec=grid_spec,
        out_shape=jax.ShapeDtypeStruct((n, D_ATTN), F32),
        compiler_params=_params(1),
        name="paged_attn",
    )(page_table.reshape(-1), q, k_new, v_new, *lams, g_subln, pool_kt, pool_v)


def _mixer_out_kernel(x_ref, o_ref, c_ref, wo_ref, g_ref, wq_ref, x1_ref, qm_ref, *, scale):
    da = o_ref.shape[1]
    x1 = (x_ref[...] + _dot(o_ref[...].astype(BF16), wo_ref[0:da, :])
          + _dot(c_ref[...], wo_ref[da:, :]))
    x1_ref[...] = x1
    xn = _rms(x1, g_ref[...]).astype(BF16)
    qm_ref[...] = (_dot(xn, wq_ref[...]) * scale).astype(qm_ref.dtype)


def _mixer_out(x, o, c, w_out, g_cross, w_q, *, scale, act_dtype):
    n, d = x.shape
    tm = min(ROW_TILE, n)
    row = lambda i: (i, 0)
    fixed = lambda i: (0, 0)
    return pl.pallas_call(
        functools.partial(_mixer_out_kernel, scale=scale),
        grid=(n // tm,),
        in_specs=[pl.BlockSpec((tm, d), row), pl.BlockSpec((tm, o.shape[1]), row),
                  pl.BlockSpec((tm, c.shape[1]), row), pl.BlockSpec((d, d), fixed),
                  pl.BlockSpec((1, d), fixed), pl.BlockSpec((d, d), fixed)],
        out_specs=[pl.BlockSpec((tm, d), row)] * 2,
        out_shape=[jax.ShapeDtypeStruct((n, d), F32), jax.ShapeDtypeStruct((n, d), act_dtype)],
        compiler_params=_params(1),
        name="mixer_out",
    )(x, o, c, w_out, g_cross, w_q)


def _cross_attn_kernel(q_ref, mk_ref, mv_ref, o_ref, *, seqs):
    dh = q_ref.shape[1] // H_MEM
    tq = q_ref.shape[0] // seqs
    mem_rows = mk_ref.shape[0] // seqs

    def head(ref, i, h):
        if ref.shape[1] == q_ref.shape[1]:
            return ref[i * mem_rows:(i + 1) * mem_rows, h * dh:(h + 1) * dh].astype(BF16)
        chunks = dh // LANES
        step = chunks * H_MEM
        return jnp.concatenate(
            [ref[pl.ds(i * mem_rows + c * H_MEM + h, mem_rows // step, stride=step), :]
             for c in range(chunks)], axis=1).astype(BF16)

    pairs = [(i, h) for i in range(seqs) for h in range(H_MEM)]
    block = lambda i, h: (slice(i * tq, (i + 1) * tq), slice(h * dh, (h + 1) * dh))
    s = [_dot_nt(q_ref[block(i, h)].astype(BF16), head(mk_ref, i, h)) for i, h in pairs]
    p = [jnp.exp(x - jnp.max(x, axis=-1, keepdims=True)) for x in s]
    l = [jnp.sum(x, axis=-1, keepdims=True) for x in p]
    for (i, h), x, y in zip(pairs, p, l):
        o_ref[block(i, h)] = (_dot(x.astype(BF16), head(mv_ref, i, h)) / y).astype(o_ref.dtype)


def _cross_attn(qm, mk, mv, *, rows_per_seq):
    n, d = qm.shape
    n_seq = n // rows_per_seq
    mem_rows = mk.shape[0] // n_seq
    if rows_per_seq >= ROW_TILE:
        tm, seqs = ROW_TILE, 1
        tps = rows_per_seq // tm
        mem = pl.BlockSpec((mem_rows, mk.shape[1]), lambda i: (i // tps, 0))
    else:
        seqs = min(SHORT_SEQS_PER_STEP, n_seq)
        tm = seqs * rows_per_seq
        mem = pl.BlockSpec((seqs * mem_rows, mk.shape[1]), lambda i: (i, 0))
    assert n % tm == 0
    return pl.pallas_call(
        functools.partial(_cross_attn_kernel, seqs=seqs),
        grid=(n // tm,),
        in_specs=[pl.BlockSpec((tm, d), lambda i: (i, 0)), mem, mem],
        out_specs=pl.BlockSpec((tm, d), lambda i: (i, 0)),
        out_shape=jax.ShapeDtypeStruct((n, d), qm.dtype),
        compiler_params=_params(1),
        name="cross_attn",
    )(qm, mk, mv)


def _mlp_kernel(x_ref, oc_ref, wo_ref, g_ref, wu_ref, wd_ref, gf_ref, y_ref, *, ff_chunk):
    x2 = x_ref[...] + _dot(oc_ref[...].astype(BF16), wo_ref[...])
    xn = _rms(x2, g_ref[...]).astype(BF16)
    acc = x2
    for c in range(wu_ref.shape[1] // ff_chunk):
        sl = slice(c * ff_chunk, (c + 1) * ff_chunk)
        h = jnp.maximum(_dot(xn, wu_ref[:, sl]), 0.0)
        acc = acc + _dot((h * h).astype(BF16), wd_ref[sl, :])
    y_ref[...] = _rms(acc, gf_ref[...])


def _mlp(x1, oc, w_o, g_mlp, w_up, w_down, g_final):
    n, d = x1.shape
    dff = w_up.shape[1]
    tm = min(ROW_TILE, n)
    row = lambda i: (i, 0)
    fixed = lambda i: (0, 0)
    return pl.pallas_call(
        functools.partial(_mlp_kernel, ff_chunk=min(1024, dff)),
        grid=(n // tm,),
        in_specs=[pl.BlockSpec((tm, d), row), pl.BlockSpec((tm, d), row),
                  pl.BlockSpec((d, d), fixed), pl.BlockSpec((1, d), fixed),
                  pl.BlockSpec((d, dff), fixed), pl.BlockSpec((dff, d), fixed),
                  pl.BlockSpec((1, d), fixed)],
        out_specs=pl.BlockSpec((tm, d), row),
        out_shape=jax.ShapeDtypeStruct((n, d), F32),
        compiler_params=_params(1),
        name="mlp",
    )(x1, oc, w_o, g_mlp, w_up, w_down, g_final)


def _rope_angles(pos):
    half = DH // 2
    inv = jnp.exp(jnp.arange(half, dtype=F32) * (-2.0 * math.log(ROPE_THETA) / DH))
    ang = pos.astype(F32)[:, None] * inv[None, :]
    return jnp.cos(ang), jnp.sin(ang)


def _rope_row_tables(pos, reps):
    cos, sin = _rope_angles(pos)
    cos = jnp.tile(jnp.concatenate([cos, cos], axis=-1), (reps, LANES // DH))
    sin = jnp.tile(jnp.concatenate([-sin, sin], axis=-1), (reps, LANES // DH))
    return cos, sin


def kernel(x_prompt, x_sample, mem_prompt, cache_k, cache_v, state_conv, cache_mem_k, cache_mem_v, page_table, g_mix, w_in, lambda_q1, lambda_k1, lambda_q2, lambda_k2, g_subln, conv_w, g_conv, w_out, g_cross, g_mem, w_q_mem, w_k_mem, w_v_mem, w_o_mem, g_mlp, w_up, w_down, g_final):
    assert w_in.shape[0] == 1, "single-layer trunk: the final RMSNorm is fused into the MLP kernel"
    b_p, s_p, d = x_prompt.shape
    b_s, t_s, _ = x_sample.shape
    n_mem = mem_prompt.shape[1]
    dc = conv_w.shape[-1]
    dh_mem = d // H_MEM
    mem_scale = dh_mem ** -0.5
    past_len = page_table.shape[1] * cache_k.shape[2]
    assert t_s >= CONV_W - 1 and (b_s * t_s) % SUBLANES == 0

    pos_p = jnp.arange(s_p)
    tabs_p = tuple(a.T for a in _rope_angles(pos_p))
    tm_s = min(ROW_TILE, b_s * t_s)
    tabs_s = _rope_row_tables(past_len + jnp.arange(t_s), tm_s // t_s)

    xp = x_prompt.reshape(b_p * s_p, d)
    xs = x_sample.reshape(b_s * t_s, d)
    mem = mem_prompt.reshape(b_p * n_mem, d)
    row = lambda a: a.reshape(1, -1)
    l = 0
    lam0 = _lambda_init(l)
    lams = [row(a[l]) for a in (lambda_q1, lambda_k1, lambda_q2, lambda_k2)]
    w_in_b = w_in[l].astype(BF16)
    w_qk_t = w_in[l][:, 0:2 * D_ATTN].T.astype(BF16)
    w_out_b = w_out[l].astype(BF16)
    w_q_b = w_q_mem[l].astype(BF16)
    w_o_b = w_o_mem[l].astype(BF16)
    w_up_b = w_up[l].astype(BF16)
    w_down_b = w_down[l].astype(BF16)

    def tail(x, o, c, mk, mv, rows_per_seq, act_dtype):
        x1, qm = _mixer_out(x, o, c, w_out_b, row(g_cross[l]), w_q_b, scale=mem_scale,
                            act_dtype=act_dtype)
        oc = _cross_attn(qm, mk, mv, rows_per_seq=rows_per_seq)
        return _mlp(x1, oc, w_o_b, row(g_mlp[l]), w_up_b, w_down_b, row(g_final))

    mk, mv, mkb, mvb = _mem_kv(mem, row(g_mem[l]), w_k_mem[l].astype(BF16),
                               w_v_mem[l].astype(BF16))
    qt, kt, kb, vp, vt, c, conv_p = _mixer_in_prompt(
        xp, row(g_mix[l]), w_in_b, w_qk_t, tabs_p, conv_w[l], row(g_conv[l]), batch=b_p, seq=s_p)
    o = _prompt_attn(qt, kb, vt, lams, row(g_subln[l]), lam0=lam0)
    yp = tail(xp, o, c, mkb, mvb, s_p, BF16)
    kp = kt.reshape(b_p, N_SUB, DH, s_p).transpose(0, 3, 1, 2)

    prev = state_conv[l]
    p1 = jnp.pad(prev[:, 1:2], ((0, 0), (0, t_s - 1), (0, 0))).reshape(b_s * t_s, dc)
    p2 = jnp.pad(prev, ((0, 0), (0, t_s - 2), (0, 0))).reshape(b_s * t_s, dc)
    q, ks, vs, c, u = _mixer_in_sample(xs, row(g_mix[l]), w_in_b, tabs_s, conv_w[l],
                                       row(g_conv[l]), (p1, p2), seq_rows=t_s)
    o = _paged_attn(q, ks, vs, lams, row(g_subln[l]), cache_k[l].transpose(0, 2, 3, 1),
                    cache_v[l].reshape(cache_v.shape[1], -1, HEAD_W), page_table, lam0=lam0)

    def lane_rows(m):
        m = m.reshape(b_s * n_mem, H_MEM, dh_mem // LANES, LANES)
        return m.transpose(0, 2, 1, 3).reshape(-1, LANES)

    ys = tail(xs, o, c, lane_rows(cache_mem_k[l]), lane_rows(cache_mem_v[l]), t_s, F32)
    conv_s = u.reshape(b_s, t_s, dc)[:, t_s - (CONV_W - 1):]

    return (yp.reshape(b_p, s_p, d), ys.reshape(b_s, t_s, d),
            kp[None], vp.reshape(1, b_p, s_p, H_A, HEAD_W), conv_p[None],
            mk.reshape(1, b_p, n_mem, H_MEM, dh_mem), mv.reshape(1, b_p, n_mem, H_MEM, dh_mem),
            ks.reshape(1, b_s, t_s, N_SUB, DH), vs.reshape(1, b_s, t_s, H_A, HEAD_W),
            conv_s[None])
```

```python
import functools
import math

import jax
import jax.numpy as jnp
from jax import lax
from jax.experimental import pallas as pl
from jax.experimental.pallas import tpu as pltpu

F32 = jnp.float32
BF16 = jnp.bfloat16

H_A = 4
DH = 64
N_SUB = 2 * H_A
HEAD_W = 2 * DH
D_ATTN = H_A * HEAD_W
CONV_W = 3
H_MEM = 4
ROPE_THETA = 10000.0
EPS = 1e-6
ATTN_SCALE = DH ** -0.5
LANES = 128
SUBLANES = 8
VMEM_LIMIT = 56 * 1024 * 1024

ONES_ROWS = 16
ROW_TILE = 512
SHORT_SEQS_PER_STEP = 4
PAGES_PER_STEP = 8


def _lambda_init(l):
    return 0.8 - 0.6 * math.exp(-0.3 * l)


def _rms(x, g):
    return x * lax.rsqrt(jnp.mean(x * x, axis=-1, keepdims=True) + EPS) * g


def _dot(a, b):
    return jnp.dot(a, b, preferred_element_type=F32)


def _dot_nt(a, b):
    return lax.dot_general(a, b, (((1,), (1,)), ((), ())), preferred_element_type=F32)


def _params(n_axes):
    return pltpu.CompilerParams(dimension_semantics=("arbitrary",) * n_axes,
                                vmem_limit_bytes=VMEM_LIMIT)


def _diff_lambda(lq1_ref, lk1_ref, lq2_ref, lk2_ref, lam0):
    a = jnp.sum(lq1_ref[...] * lk1_ref[...], axis=-1, keepdims=True)
    b = jnp.sum(lq2_ref[...] * lk2_ref[...], axis=-1, keepdims=True)
    return jnp.exp(a) - jnp.exp(b) + lam0


def _store_heads(ref, x):
    w = ref.shape[2]
    for h in range(ref.shape[1]):
        ref[:, h, :] = x[:, h * w:(h + 1) * w].astype(ref.dtype)


def _mem_kv_kernel(m_ref, g_ref, wk_ref, wv_ref, k_ref, v_ref, kb_ref, vb_ref):
    mn = _rms(m_ref[...], g_ref[...]).astype(BF16)
    k = _dot(mn, wk_ref[...])
    v = _dot(mn, wv_ref[...])
    _store_heads(k_ref, k)
    _store_heads(v_ref, v)
    kb_ref[...] = k.astype(BF16)
    vb_ref[...] = v.astype(BF16)


def _mem_kv(mem, g, wk, wv):
    n, d = mem.shape
    tm = min(ROW_TILE, n)
    row = lambda i: (i, 0)
    fixed = lambda i: (0, 0)
    heads = pl.BlockSpec((tm, H_MEM, d // H_MEM), lambda i: (i, 0, 0))
    heads_shape = jax.ShapeDtypeStruct((n, H_MEM, d // H_MEM), F32)
    return pl.pallas_call(
        _mem_kv_kernel,
        grid=(n // tm,),
        in_specs=[pl.BlockSpec((tm, d), row), pl.BlockSpec((1, d), fixed),
                  pl.BlockSpec((d, d), fixed), pl.BlockSpec((d, d), fixed)],
        out_specs=[heads, heads, pl.BlockSpec((tm, d), row), pl.BlockSpec((tm, d), row)],
        out_shape=[heads_shape, heads_shape] + [jax.ShapeDtypeStruct((n, d), BF16)] * 2,
        compiler_params=_params(1),
        name="mem_kv",
    )(mem, g, wk, wv)


def _rope_rows(p, cos, sin):
    lane = lax.broadcasted_iota(jnp.int32, (p.shape[0], LANES), 1)
    first_half = (lane % DH) < (DH // 2)
    out = []
    for c in range(D_ATTN // LANES):
        pc = p[:, c * LANES:(c + 1) * LANES]
        swapped = jnp.where(first_half, pltpu.roll(pc, LANES - DH // 2, 1),
                            pltpu.roll(pc, DH // 2, 1))
        out.append(pc * cos + swapped * sin)
    return out


def _conv_branch(xn, w_ref, cw_ref, gc_ref, fix_history):
    dc = cw_ref.shape[1]
    o0 = 3 * D_ATTN
    gate_b = _dot(xn, w_ref[:, o0:o0 + dc])
    u = _dot(xn, w_ref[:, o0 + dc:o0 + 2 * dc]) * _dot(xn, w_ref[:, o0 + 2 * dc:o0 + 3 * dc])
    row = lax.broadcasted_iota(jnp.int32, u.shape, 0)
    um1, um2 = fix_history(row, pltpu.roll(u, 1, 0), pltpu.roll(u, 2, 0))
    y = cw_ref[0:1, :] * um2 + cw_ref[1:2, :] * um1 + cw_ref[2:3, :] * u
    return _rms(gate_b * y, gc_ref[...]).astype(BF16), u


def _rope_cols(pt, cos_t, sin_t, outs):
    half = DH // 2
    for s in range(N_SUB):
        x1 = pt[s * DH:s * DH + half, :]
        x2 = pt[s * DH + half:(s + 1) * DH, :]
        r1 = x1 * cos_t - x2 * sin_t
        r2 = x2 * cos_t + x1 * sin_t
        for ref, scale in outs:
            if scale != 1.0:
                r1, r2 = r1 * scale, r2 * scale
            ref[s * DH:s * DH + half, :] = r1.astype(ref.dtype)
            ref[s * DH + half:(s + 1) * DH, :] = r2.astype(ref.dtype)


def _mixer_in_prompt_kernel(x_ref, g_ref, w_ref, wt_ref, cost_ref, sint_ref, cw_ref, gc_ref,
                            qt_ref, kt_ref, kb_ref, v_ref, vt_ref, c_ref, cs_ref, carry_ref, *,
                            tiles_per_seq):
    tm = x_ref.shape[0]
    xn = _rms(x_ref[...], g_ref[...]).astype(BF16)
    cos_t = cost_ref[...]
    sin_t = sint_ref[...]

    _rope_cols(_dot_nt(wt_ref[0:D_ATTN, :], xn), cos_t, sin_t, [(qt_ref, ATTN_SCALE)])
    _rope_cols(_dot_nt(wt_ref[D_ATTN:2 * D_ATTN, :], xn), cos_t, sin_t, [(kt_ref, 1.0)])
    kb_ref[...] = kt_ref[...].T.astype(kb_ref.dtype)
    v = _dot(xn, w_ref[:, 2 * D_ATTN:3 * D_ATTN])
    _store_heads(v_ref, v)
    vt_ref[...] = v.T.astype(vt_ref.dtype)

    @pl.when(pl.program_id(0) % tiles_per_seq == 0)
    def _():
        carry_ref[...] = jnp.zeros_like(carry_ref)

    def fix_history(row, um1, um2):
        prev0 = carry_ref[SUBLANES - 2:SUBLANES - 1, :]
        prev1 = carry_ref[SUBLANES - 1:SUBLANES, :]
        return (jnp.where(row == 0, prev1, um1),
                jnp.where(row == 0, prev0, jnp.where(row == 1, prev1, um2)))

    c, u = _conv_branch(xn, w_ref, cw_ref, gc_ref, fix_history)
    c_ref[...] = c
    carry_ref[...] = u[tm - SUBLANES:tm, :]
    cs_ref[...] = u[tm - (CONV_W - 1):tm, :]


def _mixer_in_prompt(x, g, w_in, w_qk_t, tabs, conv_w, g_conv, *, batch, seq):
    n, d = x.shape
    dc = conv_w.shape[1]
    tm = min(ROW_TILE, seq)
    assert seq % tm == 0
    tps = seq // tm
    cos_t, sin_t = tabs
    row = lambda i: (i, 0)
    fixed = lambda i: (0, 0)
    tile_t = pl.BlockSpec((None, None, D_ATTN, tm), lambda i: (i // tps, i % tps, 0, 0))
    tile_t_shape = jax.ShapeDtypeStruct((batch, tps, D_ATTN, tm), BF16)
    return pl.pallas_call(
        functools.partial(_mixer_in_prompt_kernel, tiles_per_seq=tps),
        grid=(n // tm,),
        in_specs=[pl.BlockSpec((tm, d), row), pl.BlockSpec((1, d), fixed),
                  pl.BlockSpec(w_in.shape, fixed), pl.BlockSpec(w_qk_t.shape, fixed),
                  pl.BlockSpec((DH // 2, tm), lambda i: (0, i % tps)),
                  pl.BlockSpec((DH // 2, tm), lambda i: (0, i % tps)),
                  pl.BlockSpec((CONV_W, dc), fixed), pl.BlockSpec((1, dc), fixed)],
        out_specs=[tile_t,
                   pl.BlockSpec((None, D_ATTN, tm), lambda i: (i // tps, 0, i % tps)),
                   pl.BlockSpec((tm, D_ATTN), row),
                   pl.BlockSpec((tm, H_A, HEAD_W), lambda i: (i, 0, 0)),
                   tile_t,
                   pl.BlockSpec((tm, dc), row),
                   pl.BlockSpec((None, CONV_W - 1, dc), lambda i: (i // tps, 0, 0))],
        out_shape=[tile_t_shape,
                   jax.ShapeDtypeStruct((batch, D_ATTN, seq), F32),
                   jax.ShapeDtypeStruct((n, D_ATTN), BF16),
                   jax.ShapeDtypeStruct((n, H_A, HEAD_W), F32),
                   tile_t_shape,
                   jax.ShapeDtypeStruct((n, dc), BF16),
                   jax.ShapeDtypeStruct((batch, CONV_W - 1, dc), F32)],
        scratch_shapes=[pltpu.VMEM((SUBLANES, dc), F32)],
        compiler_params=_params(1),
        name="mixer_in_prompt",
    )(x, g, w_in, w_qk_t, cos_t, sin_t, conv_w, g_conv)


def _mixer_in_sample_kernel(x_ref, g_ref, w_ref, cos_ref, sin_ref, cw_ref, gc_ref, p1_ref, p2_ref,
                            q_ref, k_ref, v_ref, c_ref, u_ref, *, seq_rows):
    xn = _rms(x_ref[...], g_ref[...]).astype(BF16)
    cos = cos_ref[...]
    sin = sin_ref[...]
    for c, r in enumerate(_rope_rows(_dot(xn, w_ref[:, 0:D_ATTN]), cos, sin)):
        q_ref[:, c * LANES:(c + 1) * LANES] = r * ATTN_SCALE
    for c, r in enumerate(_rope_rows(_dot(xn, w_ref[:, D_ATTN:2 * D_ATTN]), cos, sin)):
        k_ref[:, c * LANES:(c + 1) * LANES] = r
    v_ref[...] = _dot(xn, w_ref[:, 2 * D_ATTN:3 * D_ATTN])

    def fix_history(row, um1, um2):
        t = row % seq_rows
        return jnp.where(t == 0, p1_ref[...], um1), jnp.where(t < 2, p2_ref[...], um2)

    c, u = _conv_branch(xn, w_ref, cw_ref, gc_ref, fix_history)
    c_ref[...] = c
    u_ref[...] = u


def _mixer_in_sample(x, g, w_in, tabs, conv_w, g_conv, hist, *, seq_rows):
    n, d = x.shape
    dc = conv_w.shape[1]
    tm = min(ROW_TILE, n)
    cos, sin = tabs
    row = lambda i: (i, 0)
    fixed = lambda i: (0, 0)
    act = pl.BlockSpec((tm, D_ATTN), row)
    conv = pl.BlockSpec((tm, dc), row)
    return pl.pallas_call(
        functools.partial(_mixer_in_sample_kernel, seq_rows=seq_rows),
        grid=(n // tm,),
        in_specs=[pl.BlockSpec((tm, d), row), pl.BlockSpec((1, d), fixed),
                  pl.BlockSpec(w_in.shape, fixed),
                  pl.BlockSpec((tm, LANES), fixed), pl.BlockSpec((tm, LANES), fixed),
                  pl.BlockSpec((CONV_W, dc), fixed), pl.BlockSpec((1, dc), fixed), conv, conv],
        out_specs=[act, act, act, conv, conv],
        out_shape=[jax.ShapeDtypeStruct((n, D_ATTN), F32)] * 3
                  + [jax.ShapeDtypeStruct((n, dc), BF16), jax.ShapeDtypeStruct((n, dc), F32)],
        compiler_params=_params(1),
        name="mixer_in_sample",
    )(x, g, w_in, cos, sin, conv_w, g_conv, *hist)


def _head_out(a0, l0, a1, l1, lam, g, lam0):
    o = a0 / l0 - lam * (a1 / l1)
    return _rms(o, g) * (1.0 - lam0)


def _prompt_attn_kernel(qt_ref, k_ref, vt_ref, lq1, lk1, lq2, lk2, gs_ref, o_ref,
                        qt2_ref, s_ref, vx_ref, m_ref, acc_ref, *, lam0):
    tq = qt_ref.shape[1]
    qi = pl.program_id(2)
    qt = qt_ref[...]
    d_row = lax.broadcasted_iota(jnp.int32, qt.shape, 0)
    qt2_ref[0] = jnp.where(d_row < DH, qt, jnp.zeros_like(qt))
    qt2_ref[1] = jnp.where(d_row >= DH, qt, jnp.zeros_like(qt))
    m_ref[...] = jnp.full_like(m_ref, -jnp.inf)
    acc_ref[...] = jnp.zeros_like(acc_ref)

    @pl.when(qi == 0)
    def _():
        vx_ref[:, 0:HEAD_W, :] = vt_ref[...]
        vx_ref[:, HEAD_W:, :] = jnp.ones((vx_ref.shape[0], ONES_ROWS, tq), vx_ref.dtype)

    def scores(j, sub):
        k = k_ref[pl.ds(pl.multiple_of(j * tq, tq), tq), :]
        s_ref[sub] = _dot(k, qt2_ref[sub])

    def softmax_pv(j, sub, masked):
        st = s_ref[sub]
        if masked:
            key = lax.broadcasted_iota(jnp.int32, st.shape, 0)
            qry = lax.broadcasted_iota(jnp.int32, st.shape, 1)
            st = jnp.where(key <= qry, st, -jnp.inf)
        m_prev = m_ref[sub]
        m_new = jnp.maximum(m_prev, jnp.max(st, axis=0, keepdims=True))
        p = jnp.exp(st - m_new)
        alpha = jnp.exp(m_prev - m_new)
        acc_ref[sub] = alpha * acc_ref[sub] + _dot(vx_ref[j], p.astype(BF16))
        m_ref[sub] = m_new

    scores(0, 0)

    def body(j, carry):
        scores(j, 1)
        softmax_pv(j, 0, False)
        scores(j + 1, 0)
        softmax_pv(j, 1, False)
        return carry

    lax.fori_loop(0, qi, body, 0)
    scores(qi, 1)
    softmax_pv(qi, 0, True)
    softmax_pv(qi, 1, True)
    lam = _diff_lambda(lq1, lk1, lq2, lk2, lam0)
    a0 = acc_ref[0]
    a1 = acc_ref[1]
    ot = (a0[0:HEAD_W] / a0[HEAD_W:HEAD_W + 1]
          - lam * (a1[0:HEAD_W] / a1[HEAD_W:HEAD_W + 1]))
    o_ref[...] = (_rms(ot.T, gs_ref[...]) * (1.0 - lam0)).astype(o_ref.dtype)


def _prompt_attn(qt, kb, vt, lams, g_subln, *, lam0):
    batch, nk, _, tk = qt.shape
    seq = nk * tk
    vec = pl.BlockSpec((1, DH), lambda b, h, i: (0, 0))
    return pl.pallas_call(
        functools.partial(_prompt_attn_kernel, lam0=lam0),
        grid=(batch, H_A, nk),
        in_specs=[pl.BlockSpec((None, None, HEAD_W, tk), lambda b, h, i: (b, i, h, 0)),
                  pl.BlockSpec((seq, HEAD_W), lambda b, h, i: (b, h)),
                  pl.BlockSpec((None, nk, HEAD_W, tk), lambda b, h, i: (b, 0, h, 0)),
                  vec, vec, vec, vec,
                  pl.BlockSpec((1, HEAD_W), lambda b, h, i: (0, 0))],
        out_specs=pl.BlockSpec((tk, HEAD_W), lambda b, h, i: (b * nk + i, h)),
        out_shape=jax.ShapeDtypeStruct((batch * seq, D_ATTN), BF16),
        scratch_shapes=[pltpu.VMEM((2, HEAD_W, tk), BF16),
                        pltpu.VMEM((2, tk, tk), F32),
                        pltpu.VMEM((nk, HEAD_W + ONES_ROWS, tk), BF16),
                        pltpu.VMEM((2, 1, tk), F32),
                        pltpu.VMEM((2, HEAD_W + ONES_ROWS, tk), F32)],
        compiler_params=_params(3),
        name="prompt_attn",
    )(qt, kb, vt, *lams, g_subln)


def _paged_ops(pt_ref, kpool, vpool, kbuf, vbuf, sem, qbd_ref, m_ref, l_ref, acc_ref, *, pages, t):
    rows, dq = qbd_ref.shape
    page = kbuf.shape[-1]

    def page_copies(g, slot):
        out = []
        for j in range(pages):
            pid = pt_ref[g * pages + j]
            out.append(pltpu.make_async_copy(kpool.at[pid], kbuf.at[slot, j], sem.at[0, slot, j]))
            out.append(pltpu.make_async_copy(vpool.at[pid], vbuf.at[slot, j], sem.at[1, slot, j]))
        return out

    def start(g, slot):
        for cp in page_copies(g, slot):
            cp.start()

    def wait(g, slot):
        for cp in page_copies(g, slot):
            cp.wait()

    def update(s, value):
        m_prev = m_ref[...]
        m_new = jnp.maximum(m_prev, jnp.max(s, axis=-1, keepdims=True))
        alpha = jnp.exp(m_prev - m_new)
        p = jnp.exp(s - m_new)
        l_ref[...] = alpha * l_ref[...] + jnp.sum(p, axis=-1, keepdims=True)
        pb = p.astype(BF16)
        pv = []
        for h in range(0, H_A, 2):
            both = _dot(pb[2 * h * t:2 * (h + 2) * t, :],
                        jnp.concatenate([value(h), value(h + 1)], axis=1))
            pv += [both[0:2 * t, 0:HEAD_W], both[2 * t:4 * t, HEAD_W:]]
        acc_ref[...] = alpha * acc_ref[...] + jnp.concatenate(pv, axis=0)
        m_ref[...] = m_new

    def begin_seq(q):
        qt = jnp.concatenate([q] * N_SUB, axis=0)
        r_i = lax.broadcasted_iota(jnp.int32, (rows, dq), 0)
        c_i = lax.broadcasted_iota(jnp.int32, (rows, dq), 1)
        qbd_ref[...] = jnp.where((r_i // t) == (c_i // DH), qt, jnp.zeros_like(qt)).astype(BF16)
        m_ref[...] = jnp.full_like(m_ref, -jnp.inf)
        l_ref[...] = jnp.zeros_like(l_ref)
        acc_ref[...] = jnp.zeros_like(acc_ref)

    def chunk(slot):
        kt = jnp.concatenate([kbuf[slot, j].reshape(dq, page).astype(BF16)
                              for j in range(pages)], axis=1)
        update(_dot(qbd_ref[...], kt),
               lambda h: jnp.concatenate(
                   [vbuf[slot, j, pl.ds(h, page, stride=H_A), :].astype(BF16)
                    for j in range(pages)], axis=0))

    def end_seq(k_new, v_new, lam, g, lam0):
        pad = jnp.zeros((LANES - t, dq), F32)
        kn = jnp.concatenate([k_new, pad], axis=0).astype(BF16)
        vn = jnp.concatenate([v_new, pad], axis=0).astype(BF16)
        s = _dot_nt(qbd_ref[...], kn)
        key_t = lax.broadcasted_iota(jnp.int32, s.shape, 1)
        qry_t = lax.broadcasted_iota(jnp.int32, s.shape, 0) % t
        update(jnp.where(key_t <= qry_t, s, -jnp.inf),
               lambda h: vn[:, h * HEAD_W:(h + 1) * HEAD_W])
        acc = acc_ref[...]
        l = l_ref[...]
        return jnp.concatenate(
            [_head_out(acc[2 * h * t:(2 * h + 1) * t], l[2 * h * t:(2 * h + 1) * t],
                       acc[(2 * h + 1) * t:(2 * h + 2) * t], l[(2 * h + 1) * t:(2 * h + 2) * t],
                       lam, g, lam0) for h in range(H_A)], axis=1)

    return start, wait, begin_seq, chunk, end_seq


def _mixer_out_kernel(x_ref, o_ref, c_ref, wo_ref, g_ref, wq_ref, x1_ref, qm_ref, *, scale):
    da = o_ref.shape[1]
    x1 = (x_ref[...] + _dot(o_ref[...].astype(BF16), wo_ref[0:da, :])
          + _dot(c_ref[...], wo_ref[da:, :]))
    x1_ref[...] = x1
    xn = _rms(x1, g_ref[...]).astype(BF16)
    qm_ref[...] = (_dot(xn, wq_ref[...]) * scale).astype(qm_ref.dtype)


def _mixer_out(x, o, c, w_out, g_cross, w_q, *, scale, act_dtype):
    n, d = x.shape
    tm = min(ROW_TILE, n)
    row = lambda i: (i, 0)
    fixed = lambda i: (0, 0)
    return pl.pallas_call(
        functools.partial(_mixer_out_kernel, scale=scale),
        grid=(n // tm,),
        in_specs=[pl.BlockSpec((tm, d), row), pl.BlockSpec((tm, o.shape[1]), row),
                  pl.BlockSpec((tm, c.shape[1]), row), pl.BlockSpec((d, d), fixed),
                  pl.BlockSpec((1, d), fixed), pl.BlockSpec((d, d), fixed)],
        out_specs=[pl.BlockSpec((tm, d), row)] * 2,
        out_shape=[jax.ShapeDtypeStruct((n, d), F32), jax.ShapeDtypeStruct((n, d), act_dtype)],
        compiler_params=_params(1),
        name="mixer_out",
    )(x, o, c, w_out, g_cross, w_q)


def _cross_attn_kernel(q_ref, mk_ref, mv_ref, o_ref, *, seqs):
    dh = q_ref.shape[1] // H_MEM
    tq = q_ref.shape[0] // seqs
    mem_rows = mk_ref.shape[0] // seqs

    def head(ref, i, h):
        if ref.shape[1] == q_ref.shape[1]:
            return ref[i * mem_rows:(i + 1) * mem_rows, h * dh:(h + 1) * dh].astype(BF16)
        chunks = dh // LANES
        step = chunks * H_MEM
        return jnp.concatenate(
            [ref[pl.ds(i * mem_rows + c * H_MEM + h, mem_rows // step, stride=step), :]
             for c in range(chunks)], axis=1).astype(BF16)

    pairs = [(i, h) for i in range(seqs) for h in range(H_MEM)]
    block = lambda i, h: (slice(i * tq, (i + 1) * tq), slice(h * dh, (h + 1) * dh))
    s = [_dot_nt(q_ref[block(i, h)].astype(BF16), head(mk_ref, i, h)) for i, h in pairs]
    p = [jnp.exp(x - jnp.max(x, axis=-1, keepdims=True)) for x in s]
    l = [jnp.sum(x, axis=-1, keepdims=True) for x in p]
    for (i, h), x, y in zip(pairs, p, l):
        o_ref[block(i, h)] = (_dot(x.astype(BF16), head(mv_ref, i, h)) / y).astype(o_ref.dtype)


def _cross_attn(qm, mk, mv, *, rows_per_seq):
    n, d = qm.shape
    n_seq = n // rows_per_seq
    mem_rows = mk.shape[0] // n_seq
    if rows_per_seq >= ROW_TILE:
        tm, seqs = ROW_TILE, 1
        tps = rows_per_seq // tm
        mem = pl.BlockSpec((mem_rows, mk.shape[1]), lambda i: (i // tps, 0))
    else:
        seqs = min(SHORT_SEQS_PER_STEP, n_seq)
        tm = seqs * rows_per_seq
        mem = pl.BlockSpec((seqs * mem_rows, mk.shape[1]), lambda i: (i, 0))
    assert n % tm == 0
    return pl.pallas_call(
        functools.partial(_cross_attn_kernel, seqs=seqs),
        grid=(n // tm,),
        in_specs=[pl.BlockSpec((tm, d), lambda i: (i, 0)), mem, mem],
        out_specs=pl.BlockSpec((tm, d), lambda i: (i, 0)),
        out_shape=jax.ShapeDtypeStruct((n, d), qm.dtype),
        compiler_params=_params(1),
        name="cross_attn",
    )(qm, mk, mv)


def _mlp_kernel(x_ref, oc_ref, wo_ref, g_ref, wu_ref, wd_ref, gf_ref, y_ref, *, ff_chunk):
    x2 = x_ref[...] + _dot(oc_ref[...].astype(BF16), wo_ref[...])
    xn = _rms(x2, g_ref[...]).astype(BF16)
    acc = x2
    for c in range(wu_ref.shape[1] // ff_chunk):
        sl = slice(c * ff_chunk, (c + 1) * ff_chunk)
        h = jnp.maximum(_dot(xn, wu_ref[:, sl]), 0.0)
        acc = acc + _dot((h * h).astype(BF16), wd_ref[sl, :])
    y_ref[...] = _rms(acc, gf_ref[...])


def _mlp(x1, oc, w_o, g_mlp, w_up, w_down, g_final):
    n, d = x1.shape
    dff = w_up.shape[1]
    tm = min(ROW_TILE, n)
    row = lambda i: (i, 0)
    fixed = lambda i: (0, 0)
    return pl.pallas_call(
        functools.partial(_mlp_kernel, ff_chunk=min(1024, dff)),
        grid=(n // tm,),
        in_specs=[pl.BlockSpec((tm, d), row), pl.BlockSpec((tm, d), row),
                  pl.BlockSpec((d, d), fixed), pl.BlockSpec((1, d), fixed),
                  pl.BlockSpec((d, dff), fixed), pl.BlockSpec((dff, d), fixed),
                  pl.BlockSpec((1, d), fixed)],
        out_specs=pl.BlockSpec((tm, d), row),
        out_shape=jax.ShapeDtypeStruct((n, d), F32),
        compiler_params=_params(1),
        name="mlp",
    )(x1, oc, w_o, g_mlp, w_up, w_down, g_final)


def _mlp_paged_kernel(pt_ref, x_ref, oc_ref, wo_ref, g_ref, wu_ref, wd_ref, gf_ref,
                      q_ref, kn_ref, vn_ref, lq1, lk1, lq2, lk2, gs_ref, kpool, vpool,
                      y_ref, o_ref, kbuf, vbuf, sem, qbd_ref, m_ref, l_ref, acc_ref, *,
                      ff_chunk, pages, chunks_per_seq, lam0):
    i = pl.program_id(0)
    t = q_ref.shape[0]
    start, wait, begin_seq, chunk, end_seq = _paged_ops(
        pt_ref, kpool, vpool, kbuf, vbuf, sem, qbd_ref, m_ref, l_ref, acc_ref, pages=pages, t=t)
    g0 = i * chunks_per_seq

    @pl.when(i == 0)
    def _():
        start(0, 0)

    x2 = x_ref[...] + _dot(oc_ref[...].astype(BF16), wo_ref[...])
    xn = _rms(x2, g_ref[...]).astype(BF16)
    acc = x2
    n_ff = wu_ref.shape[1] // ff_chunk
    ff_done = 0
    begin_seq(q_ref[...])
    for c in range(chunks_per_seq):
        if c + 1 < chunks_per_seq:
            start(g0 + c + 1, (c + 1) % 2)
        else:
            @pl.when(i + 1 < pl.num_programs(0))
            def _():
                start(g0 + chunks_per_seq, 0)
        wait(g0 + c, c % 2)
        chunk(c % 2)
        while ff_done < (c + 1) * n_ff // chunks_per_seq:
            sl = slice(ff_done * ff_chunk, (ff_done + 1) * ff_chunk)
            h = jnp.maximum(_dot(xn, wu_ref[:, sl]), 0.0)
            acc = acc + _dot((h * h).astype(BF16), wd_ref[sl, :])
            ff_done += 1
    lam = _diff_lambda(lq1, lk1, lq2, lk2, lam0)
    o_ref[...] = end_seq(kn_ref[...], vn_ref[...], lam, gs_ref[...], lam0).astype(o_ref.dtype)
    y_ref[...] = _rms(acc, gf_ref[...])


def _mlp_paged(x1, oc, w_o, g_mlp, w_up, w_down, g_final,
               q, k_new, v_new, lams, g_subln, pool_kt, pool_v, page_table, *, lam0):
    n, d = x1.shape
    dff = w_up.shape[1]
    tm = min(ROW_TILE, n)
    n_seq, n_pages = page_table.shape
    t = q.shape[0] // n_seq
    page = pool_kt.shape[3]
    pages = min(PAGES_PER_STEP, n_pages)
    cps = n_pages // pages
    assert n_seq == n // tm, "one sample sequence per MLP row tile"
    assert n_pages % pages == 0 and cps % 2 == 0
    row = lambda i, pt: (i, 0)
    fixed = lambda i, pt: (0, 0)
    once = dict(pipeline_mode=pl.Buffered(1))
    seq = pl.BlockSpec((t, D_ATTN), row)
    vec = pl.BlockSpec((1, DH), fixed)
    rows = N_SUB * t
    grid_spec = pltpu.PrefetchScalarGridSpec(
        num_scalar_prefetch=1,
        grid=(n // tm,),
        in_specs=[pl.BlockSpec((tm, d), row), pl.BlockSpec((tm, d), row),
                  pl.BlockSpec((d, d), fixed, **once), pl.BlockSpec((1, d), fixed),
                  pl.BlockSpec((d, dff), fixed, **once), pl.BlockSpec((dff, d), fixed, **once),
                  pl.BlockSpec((1, d), fixed),
                  seq, seq, seq, vec, vec, vec, vec, pl.BlockSpec((1, HEAD_W), fixed),
                  pl.BlockSpec(memory_space=pl.ANY), pl.BlockSpec(memory_space=pl.ANY)],
        out_specs=[pl.BlockSpec((tm, d), row), seq],
        scratch_shapes=[pltpu.VMEM((2, pages, N_SUB, DH, page), F32),
                        pltpu.VMEM((2, pages, page * H_A, HEAD_W), F32),
                        pltpu.SemaphoreType.DMA((2, 2, pages)),
                        pltpu.VMEM((rows, D_ATTN), BF16),
                        pltpu.VMEM((rows, 1), F32), pltpu.VMEM((rows, 1), F32),
                        pltpu.VMEM((rows, HEAD_W), F32)],
    )
    return pl.pallas_call(
        functools.partial(_mlp_paged_kernel, ff_chunk=min(1024, dff), pages=pages,
                          chunks_per_seq=cps, lam0=lam0),
        grid_spec=grid_spec,
        out_shape=[jax.ShapeDtypeStruct((n, d), F32),
                   jax.ShapeDtypeStruct((n_seq * t, D_ATTN), F32)],
        compiler_params=_params(1),
        name="mlp_paged",
    )(page_table.reshape(-1), x1, oc, w_o, g_mlp, w_up, w_down, g_final,
      q, k_new, v_new, *lams, g_subln, pool_kt, pool_v)


def _rope_angles(pos):
    half = DH // 2
    inv = jnp.exp(jnp.arange(half, dtype=F32) * (-2.0 * math.log(ROPE_THETA) / DH))
    ang = pos.astype(F32)[:, None] * inv[None, :]
    return jnp.cos(ang), jnp.sin(ang)


def _rope_row_tables(pos, reps):
    cos, sin = _rope_angles(pos)
    cos = jnp.tile(jnp.concatenate([cos, cos], axis=-1), (reps, LANES // DH))
    sin = jnp.tile(jnp.concatenate([-sin, sin], axis=-1), (reps, LANES // DH))
    return cos, sin


def kernel(x_prompt, x_sample, mem_prompt, cache_k, cache_v, state_conv, cache_mem_k, cache_mem_v, page_table, g_mix, w_in, lambda_q1, lambda_k1, lambda_q2, lambda_k2, g_subln, conv_w, g_conv, w_out, g_cross, g_mem, w_q_mem, w_k_mem, w_v_mem, w_o_mem, g_mlp, w_up, w_down, g_final):
    assert w_in.shape[0] == 1, "single-layer trunk: the final RMSNorm is fused into the MLP kernel"
    b_p, s_p, d = x_prompt.shape
    b_s, t_s, _ = x_sample.shape
    n_mem = mem_prompt.shape[1]
    dc = conv_w.shape[-1]
    dh_mem = d // H_MEM
    mem_scale = dh_mem ** -0.5
    past_len = page_table.shape[1] * cache_k.shape[2]
    assert t_s >= CONV_W - 1 and (b_s * t_s) % SUBLANES == 0

    pos_p = jnp.arange(s_p)
    tabs_p = tuple(a.T for a in _rope_angles(pos_p))
    tm_s = min(ROW_TILE, b_s * t_s)
    tabs_s = _rope_row_tables(past_len + jnp.arange(t_s), tm_s // t_s)

    xp = x_prompt.reshape(b_p * s_p, d)
    xs = x_sample.reshape(b_s * t_s, d)
    mem = mem_prompt.reshape(b_p * n_mem, d)
    row = lambda a: a.reshape(1, -1)
    l = 0
    lam0 = _lambda_init(l)
    lams = [row(a[l]) for a in (lambda_q1, lambda_k1, lambda_q2, lambda_k2)]
    w_in_b = w_in[l].astype(BF16)
    w_qk_t = w_in[l][:, 0:2 * D_ATTN].T.astype(BF16)
    w_out_b = w_out[l].astype(BF16)
    w_q_b = w_q_mem[l].astype(BF16)
    w_o_b = w_o_mem[l].astype(BF16)
    w_up_b = w_up[l].astype(BF16)
    w_down_b = w_down[l].astype(BF16)

    def pre_mlp(x, o, c, mk, mv, rows_per_seq, act_dtype):
        x1, qm = _mixer_out(x, o, c, w_out_b, row(g_cross[l]), w_q_b, scale=mem_scale,
                            act_dtype=act_dtype)
        return x1, _cross_attn(qm, mk, mv, rows_per_seq=rows_per_seq)

    mlp_weights = (w_o_b, row(g_mlp[l]), w_up_b, w_down_b, row(g_final))

    mk, mv, mkb, mvb = _mem_kv(mem, row(g_mem[l]), w_k_mem[l].astype(BF16),
                               w_v_mem[l].astype(BF16))
    qt, kt, kb, vp, vt, c, conv_p = _mixer_in_prompt(
        xp, row(g_mix[l]), w_in_b, w_qk_t, tabs_p, conv_w[l], row(g_conv[l]), batch=b_p, seq=s_p)
    o = _prompt_attn(qt, kb, vt, lams, row(g_subln[l]), lam0=lam0)
    x1p, ocp = pre_mlp(xp, o, c, mkb, mvb, s_p, BF16)
    kp = kt.reshape(b_p, N_SUB, DH, s_p).transpose(0, 3, 1, 2)

    prev = state_conv[l]
    p1 = jnp.pad(prev[:, 1:2], ((0, 0), (0, t_s - 1), (0, 0))).reshape(b_s * t_s, dc)
    p2 = jnp.pad(prev, ((0, 0), (0, t_s - 2), (0, 0))).reshape(b_s * t_s, dc)
    q, ks, vs, c, u = _mixer_in_sample(xs, row(g_mix[l]), w_in_b, tabs_s, conv_w[l],
                                       row(g_conv[l]), (p1, p2), seq_rows=t_s)

    yp, o = _mlp_paged(x1p, ocp, *mlp_weights, q, ks, vs, lams, row(g_subln[l]),
                       cache_k[l].transpose(0, 2, 3, 1),
                       cache_v[l].reshape(cache_v.shape[1], -1, HEAD_W), page_table, lam0=lam0)

    def lane_rows(m):
        m = m.reshape(b_s * n_mem, H_MEM, dh_mem // LANES, LANES)
        return m.transpose(0, 2, 1, 3).reshape(-1, LANES)

    x1s, ocs = pre_mlp(xs, o, c, lane_rows(cache_mem_k[l]), lane_rows(cache_mem_v[l]), t_s, F32)
    ys = _mlp(x1s, ocs, *mlp_weights)
    conv_s = u.reshape(b_s, t_s, dc)[:, t_s - (CONV_W - 1):]

    return (yp.reshape(b_p, s_p, d), ys.reshape(b_s, t_s, d),
            kp[None], vp.reshape(1, b_p, s_p, H_A, HEAD_W), conv_p[None],
            mk.reshape(1, b_p, n_mem, H_MEM, dh_mem), mv.reshape(1, b_p, n_mem, H_MEM, dh_mem),
            ks.reshape(1, b_s, t_s, N_SUB, DH), vs.reshape(1, b_s, t_s, H_A, HEAD_W),
            conv_s[None])
```

```python
import functools
import math

import jax
import jax.numpy as jnp
from jax import lax
from jax.experimental import pallas as pl
from jax.experimental.pallas import tpu as pltpu

F32 = jnp.float32
BF16 = jnp.bfloat16

H_A = 4
DH = 64
N_SUB = 2 * H_A
HEAD_W = 2 * DH
D_ATTN = H_A * HEAD_W
CONV_W = 3
H_MEM = 4
ROPE_THETA = 10000.0
EPS = 1e-6
ATTN_SCALE = DH ** -0.5
LANES = 128
SUBLANES = 8
VMEM_LIMIT = 56 * 1024 * 1024

ONES_ROWS = 16
ROW_TILE = 512
SHORT_SEQS_PER_STEP = 4
FUSED_FF_CHUNK = 512
PAGES_PER_STEP = 8


def _lambda_init(l):
    return 0.8 - 0.6 * math.exp(-0.3 * l)


def _rms(x, g):
    return x * lax.rsqrt(jnp.mean(x * x, axis=-1, keepdims=True) + EPS) * g


def _dot(a, b):
    return jnp.dot(a, b, preferred_element_type=F32)


def _dot_nt(a, b):
    return lax.dot_general(a, b, (((1,), (1,)), ((), ())), preferred_element_type=F32)


def _params(n_axes):
    return pltpu.CompilerParams(dimension_semantics=("arbitrary",) * n_axes,
                                vmem_limit_bytes=VMEM_LIMIT)


def _diff_lambda(lq1_ref, lk1_ref, lq2_ref, lk2_ref, lam0):
    a = jnp.sum(lq1_ref[...] * lk1_ref[...], axis=-1, keepdims=True)
    b = jnp.sum(lq2_ref[...] * lk2_ref[...], axis=-1, keepdims=True)
    return jnp.exp(a) - jnp.exp(b) + lam0


def _store_heads(ref, x):
    w = ref.shape[2]
    for h in range(ref.shape[1]):
        ref[:, h, :] = x[:, h * w:(h + 1) * w].astype(ref.dtype)


def _mem_kv_kernel(m_ref, g_ref, wk_ref, wv_ref, k_ref, v_ref, kb_ref, vb_ref):
    mn = _rms(m_ref[...], g_ref[...]).astype(BF16)
    k = _dot(mn, wk_ref[...])
    v = _dot(mn, wv_ref[...])
    _store_heads(k_ref, k)
    _store_heads(v_ref, v)
    kb_ref[...] = k.astype(BF16)
    vb_ref[...] = v.astype(BF16)


def _mem_kv(mem, g, wk, wv):
    n, d = mem.shape
    tm = min(ROW_TILE, n)
    row = lambda i: (i, 0)
    fixed = lambda i: (0, 0)
    heads = pl.BlockSpec((tm, H_MEM, d // H_MEM), lambda i: (i, 0, 0))
    heads_shape = jax.ShapeDtypeStruct((n, H_MEM, d // H_MEM), F32)
    return pl.pallas_call(
        _mem_kv_kernel,
        grid=(n // tm,),
        in_specs=[pl.BlockSpec((tm, d), row), pl.BlockSpec((1, d), fixed),
                  pl.BlockSpec((d, d), fixed), pl.BlockSpec((d, d), fixed)],
        out_specs=[heads, heads, pl.BlockSpec((tm, d), row), pl.BlockSpec((tm, d), row)],
        out_shape=[heads_shape, heads_shape] + [jax.ShapeDtypeStruct((n, d), BF16)] * 2,
        compiler_params=_params(1),
        name="mem_kv",
    )(mem, g, wk, wv)


def _rope_rows(p, cos, sin):
    lane = lax.broadcasted_iota(jnp.int32, (p.shape[0], LANES), 1)
    first_half = (lane % DH) < (DH // 2)
    out = []
    for c in range(D_ATTN // LANES):
        pc = p[:, c * LANES:(c + 1) * LANES]
        swapped = jnp.where(first_half, pltpu.roll(pc, LANES - DH // 2, 1),
                            pltpu.roll(pc, DH // 2, 1))
        out.append(pc * cos + swapped * sin)
    return out


def _conv_branch(xn, w_ref, cw_ref, gc_ref, fix_history):
    dc = cw_ref.shape[1]
    o0 = 3 * D_ATTN
    gate_b = _dot(xn, w_ref[:, o0:o0 + dc])
    u = _dot(xn, w_ref[:, o0 + dc:o0 + 2 * dc]) * _dot(xn, w_ref[:, o0 + 2 * dc:o0 + 3 * dc])
    row = lax.broadcasted_iota(jnp.int32, u.shape, 0)
    um1, um2 = fix_history(row, pltpu.roll(u, 1, 0), pltpu.roll(u, 2, 0))
    y = cw_ref[0:1, :] * um2 + cw_ref[1:2, :] * um1 + cw_ref[2:3, :] * u
    return _rms(gate_b * y, gc_ref[...]).astype(BF16), u


def _rope_cols(pt, cos_t, sin_t, outs):
    half = DH // 2
    for s in range(N_SUB):
        x1 = pt[s * DH:s * DH + half, :]
        x2 = pt[s * DH + half:(s + 1) * DH, :]
        r1 = x1 * cos_t - x2 * sin_t
        r2 = x2 * cos_t + x1 * sin_t
        for ref, scale in outs:
            if scale != 1.0:
                r1, r2 = r1 * scale, r2 * scale
            ref[s * DH:s * DH + half, :] = r1.astype(ref.dtype)
            ref[s * DH + half:(s + 1) * DH, :] = r2.astype(ref.dtype)


def _mixer_in_prompt_kernel(x_ref, g_ref, w_ref, wt_ref, cost_ref, sint_ref, cw_ref, gc_ref,
                            qt_ref, kt_ref, kb_ref, v_ref, vt_ref, c_ref, cs_ref, carry_ref, *,
                            tiles_per_seq):
    tm = x_ref.shape[0]
    xn = _rms(x_ref[...], g_ref[...]).astype(BF16)
    cos_t = cost_ref[...]
    sin_t = sint_ref[...]

    _rope_cols(_dot_nt(wt_ref[0:D_ATTN, :], xn), cos_t, sin_t, [(qt_ref, ATTN_SCALE)])
    _rope_cols(_dot_nt(wt_ref[D_ATTN:2 * D_ATTN, :], xn), cos_t, sin_t, [(kt_ref, 1.0)])
    kb_ref[...] = kt_ref[...].T.astype(kb_ref.dtype)
    v = _dot(xn, w_ref[:, 2 * D_ATTN:3 * D_ATTN])
    _store_heads(v_ref, v)
    vt_ref[...] = v.T.astype(vt_ref.dtype)

    @pl.when(pl.program_id(0) % tiles_per_seq == 0)
    def _():
        carry_ref[...] = jnp.zeros_like(carry_ref)

    def fix_history(row, um1, um2):
        prev0 = carry_ref[SUBLANES - 2:SUBLANES - 1, :]
        prev1 = carry_ref[SUBLANES - 1:SUBLANES, :]
        return (jnp.where(row == 0, prev1, um1),
                jnp.where(row == 0, prev0, jnp.where(row == 1, prev1, um2)))

    c, u = _conv_branch(xn, w_ref, cw_ref, gc_ref, fix_history)
    c_ref[...] = c
    carry_ref[...] = u[tm - SUBLANES:tm, :]
    cs_ref[...] = u[tm - (CONV_W - 1):tm, :]


def _mixer_in_prompt(x, g, w_in, w_qk_t, tabs, conv_w, g_conv, *, batch, seq):
    n, d = x.shape
    dc = conv_w.shape[1]
    tm = min(ROW_TILE, seq)
    assert seq % tm == 0
    tps = seq // tm
    cos_t, sin_t = tabs
    row = lambda i: (i, 0)
    fixed = lambda i: (0, 0)
    tile_t = pl.BlockSpec((None, None, D_ATTN, tm), lambda i: (i // tps, i % tps, 0, 0))
    tile_t_shape = jax.ShapeDtypeStruct((batch, tps, D_ATTN, tm), BF16)
    return pl.pallas_call(
        functools.partial(_mixer_in_prompt_kernel, tiles_per_seq=tps),
        grid=(n // tm,),
        in_specs=[pl.BlockSpec((tm, d), row), pl.BlockSpec((1, d), fixed),
                  pl.BlockSpec(w_in.shape, fixed), pl.BlockSpec(w_qk_t.shape, fixed),
                  pl.BlockSpec((DH // 2, tm), lambda i: (0, i % tps)),
                  pl.BlockSpec((DH // 2, tm), lambda i: (0, i % tps)),
                  pl.BlockSpec((CONV_W, dc), fixed), pl.BlockSpec((1, dc), fixed)],
        out_specs=[tile_t,
                   pl.BlockSpec((None, D_ATTN, tm), lambda i: (i // tps, 0, i % tps)),
                   pl.BlockSpec((tm, D_ATTN), row),
                   pl.BlockSpec((tm, H_A, HEAD_W), lambda i: (i, 0, 0)),
                   tile_t,
                   pl.BlockSpec((tm, dc), row),
                   pl.BlockSpec((None, CONV_W - 1, dc), lambda i: (i // tps, 0, 0))],
        out_shape=[tile_t_shape,
                   jax.ShapeDtypeStruct((batch, D_ATTN, seq), F32),
                   jax.ShapeDtypeStruct((n, D_ATTN), BF16),
                   jax.ShapeDtypeStruct((n, H_A, HEAD_W), F32),
                   tile_t_shape,
                   jax.ShapeDtypeStruct((n, dc), BF16),
                   jax.ShapeDtypeStruct((batch, CONV_W - 1, dc), F32)],
        scratch_shapes=[pltpu.VMEM((SUBLANES, dc), F32)],
        compiler_params=_params(1),
        name="mixer_in_prompt",
    )(x, g, w_in, w_qk_t, cos_t, sin_t, conv_w, g_conv)


def _mixer_in_sample_kernel(x_ref, g_ref, w_ref, cos_ref, sin_ref, cw_ref, gc_ref, p1_ref, p2_ref,
                            q_ref, k_ref, v_ref, c_ref, u_ref, *, seq_rows):
    xn = _rms(x_ref[...], g_ref[...]).astype(BF16)
    cos = cos_ref[...]
    sin = sin_ref[...]
    for c, r in enumerate(_rope_rows(_dot(xn, w_ref[:, 0:D_ATTN]), cos, sin)):
        q_ref[:, c * LANES:(c + 1) * LANES] = r * ATTN_SCALE
    for c, r in enumerate(_rope_rows(_dot(xn, w_ref[:, D_ATTN:2 * D_ATTN]), cos, sin)):
        k_ref[:, c * LANES:(c + 1) * LANES] = r
    v_ref[...] = _dot(xn, w_ref[:, 2 * D_ATTN:3 * D_ATTN])

    def fix_history(row, um1, um2):
        t = row % seq_rows
        return jnp.where(t == 0, p1_ref[...], um1), jnp.where(t < 2, p2_ref[...], um2)

    c, u = _conv_branch(xn, w_ref, cw_ref, gc_ref, fix_history)
    c_ref[...] = c
    u_ref[...] = u


def _mixer_in_sample(x, g, w_in, tabs, conv_w, g_conv, hist, *, seq_rows):
    n, d = x.shape
    dc = conv_w.shape[1]
    tm = min(ROW_TILE, n)
    cos, sin = tabs
    row = lambda i: (i, 0)
    fixed = lambda i: (0, 0)
    act = pl.BlockSpec((tm, D_ATTN), row)
    conv = pl.BlockSpec((tm, dc), row)
    return pl.pallas_call(
        functools.partial(_mixer_in_sample_kernel, seq_rows=seq_rows),
        grid=(n // tm,),
        in_specs=[pl.BlockSpec((tm, d), row), pl.BlockSpec((1, d), fixed),
                  pl.BlockSpec(w_in.shape, fixed),
                  pl.BlockSpec((tm, LANES), fixed), pl.BlockSpec((tm, LANES), fixed),
                  pl.BlockSpec((CONV_W, dc), fixed), pl.BlockSpec((1, dc), fixed), conv, conv],
        out_specs=[act, act, act, conv, conv],
        out_shape=[jax.ShapeDtypeStruct((n, D_ATTN), F32)] * 3
                  + [jax.ShapeDtypeStruct((n, dc), BF16), jax.ShapeDtypeStruct((n, dc), F32)],
        compiler_params=_params(1),
        name="mixer_in_sample",
    )(x, g, w_in, cos, sin, conv_w, g_conv, *hist)


def _head_out(a0, l0, a1, l1, lam, g, lam0):
    o = a0 / l0 - lam * (a1 / l1)
    return _rms(o, g) * (1.0 - lam0)


def _prompt_attn_kernel(qt_ref, k_ref, vt_ref, lq1, lk1, lq2, lk2, gs_ref, o_ref,
                        qt2_ref, s_ref, vx_ref, m_ref, acc_ref, *, lam0):
    nk, _, tq = qt_ref.shape
    vx_ref[:, 0:HEAD_W, :] = vt_ref[...]
    vx_ref[:, HEAD_W:, :] = jnp.ones((nk, ONES_ROWS, tq), vx_ref.dtype)
    lam = _diff_lambda(lq1, lk1, lq2, lk2, lam0)

    def scores(qi, j, sub):
        k = k_ref[pl.ds(pl.multiple_of(j * tq, tq), tq), :]
        s_ref[sub] = _dot(k, qt2_ref[qi % 2, sub])

    def softmax_pv(qi, j, sub, masked):
        st = s_ref[sub]
        if masked:
            key = lax.broadcasted_iota(jnp.int32, st.shape, 0)
            qry = lax.broadcasted_iota(jnp.int32, st.shape, 1)
            st = jnp.where(key <= qry, st, -jnp.inf)
        m_prev = m_ref[qi % 2, sub]
        m_new = jnp.maximum(m_prev, jnp.max(st, axis=0, keepdims=True))
        p = jnp.exp(st - m_new)
        alpha = jnp.exp(m_prev - m_new)
        acc_ref[qi % 2, sub] = alpha * acc_ref[qi % 2, sub] + _dot(vx_ref[j], p.astype(BF16))
        m_ref[qi % 2, sub] = m_new

    def head(qi):
        qt = qt_ref[qi]
        d_row = lax.broadcasted_iota(jnp.int32, qt.shape, 0)
        qt2_ref[qi % 2, 0] = jnp.where(d_row < DH, qt, jnp.zeros_like(qt))
        qt2_ref[qi % 2, 1] = jnp.where(d_row >= DH, qt, jnp.zeros_like(qt))
        m_ref[qi % 2] = jnp.full(m_ref.shape[1:], -jnp.inf, F32)
        acc_ref[qi % 2] = jnp.zeros(acc_ref.shape[1:], F32)
        scores(qi, 0, 0)

    def finish(qi):
        a0 = acc_ref[qi % 2, 0]
        a1 = acc_ref[qi % 2, 1]
        ot = (a0[0:HEAD_W] / a0[HEAD_W:HEAD_W + 1]
              - lam * (a1[0:HEAD_W] / a1[HEAD_W:HEAD_W + 1]))
        o_ref[qi * tq:(qi + 1) * tq, :] = (_rms(ot.T, gs_ref[...])
                                           * (1.0 - lam0)).astype(o_ref.dtype)

    head(0)
    for qi in range(nk):
        def body(j, carry, qi=qi):
            scores(qi, j, 1)
            softmax_pv(qi, j, 0, False)
            scores(qi, j + 1, 0)
            softmax_pv(qi, j, 1, False)
            return carry

        lax.fori_loop(0, qi, body, 0)
        scores(qi, qi, 1)
        softmax_pv(qi, qi, 0, True)
        if qi + 1 < nk:
            head(qi + 1)
        softmax_pv(qi, qi, 1, True)
        finish(qi)


def _prompt_attn(qt, kb, vt, lams, g_subln, *, lam0):
    batch, nk, _, tk = qt.shape
    seq = nk * tk
    vec = pl.BlockSpec((1, DH), lambda b, h: (0, 0))
    tiles_t = pl.BlockSpec((None, nk, HEAD_W, tk), lambda b, h: (b, 0, h, 0))
    return pl.pallas_call(
        functools.partial(_prompt_attn_kernel, lam0=lam0),
        grid=(batch, H_A),
        in_specs=[tiles_t, pl.BlockSpec((seq, HEAD_W), lambda b, h: (b, h)), tiles_t,
                  vec, vec, vec, vec, pl.BlockSpec((1, HEAD_W), lambda b, h: (0, 0))],
        out_specs=pl.BlockSpec((seq, HEAD_W), lambda b, h: (b, h)),
        out_shape=jax.ShapeDtypeStruct((batch * seq, D_ATTN), BF16),
        scratch_shapes=[pltpu.VMEM((2, 2, HEAD_W, tk), BF16),
                        pltpu.VMEM((2, tk, tk), F32),
                        pltpu.VMEM((nk, HEAD_W + ONES_ROWS, tk), BF16),
                        pltpu.VMEM((2, 2, 1, tk), F32),
                        pltpu.VMEM((2, 2, HEAD_W + ONES_ROWS, tk), F32)],
        compiler_params=_params(2),
        name="prompt_attn",
    )(qt, kb, vt, *lams, g_subln)


def _paged_ops(pt_ref, kpool, vpool, kbuf, vbuf, sem, qbd_ref, m_ref, l_ref, acc_ref, *, pages, t):
    rows, dq = qbd_ref.shape
    page = kbuf.shape[-1]

    def page_copies(g, slot):
        out = []
        for j in range(pages):
            pid = pt_ref[g * pages + j]
            out.append(pltpu.make_async_copy(kpool.at[pid], kbuf.at[slot, j], sem.at[0, slot, j]))
            out.append(pltpu.make_async_copy(vpool.at[pid], vbuf.at[slot, j], sem.at[1, slot, j]))
        return out

    def start(g, slot):
        for cp in page_copies(g, slot):
            cp.start()

    def wait(g, slot):
        for cp in page_copies(g, slot):
            cp.wait()

    def softmax_stats(s):
        m_prev = m_ref[...]
        m_new = jnp.maximum(m_prev, jnp.max(s, axis=-1, keepdims=True))
        alpha = jnp.exp(m_prev - m_new)
        p = jnp.exp(s - m_new)
        l_ref[...] = alpha * l_ref[...] + jnp.sum(p, axis=-1, keepdims=True)
        m_ref[...] = m_new
        return p.astype(BF16), alpha

    def pv_update(pb, alpha, value):
        pv = []
        for h in range(0, H_A, 2):
            both = _dot(pb[2 * h * t:2 * (h + 2) * t, :],
                        jnp.concatenate([value(h), value(h + 1)], axis=1))
            pv += [both[0:2 * t, 0:HEAD_W], both[2 * t:4 * t, HEAD_W:]]
        acc_ref[...] = alpha * acc_ref[...] + jnp.concatenate(pv, axis=0)

    def update(s, value):
        pb, alpha = softmax_stats(s)
        pv_update(pb, alpha, value)

    def begin_seq(q):
        qt = jnp.concatenate([q] * N_SUB, axis=0)
        r_i = lax.broadcasted_iota(jnp.int32, (rows, dq), 0)
        c_i = lax.broadcasted_iota(jnp.int32, (rows, dq), 1)
        qbd_ref[...] = jnp.where((r_i // t) == (c_i // DH), qt, jnp.zeros_like(qt)).astype(BF16)
        m_ref[...] = jnp.full_like(m_ref, -jnp.inf)
        l_ref[...] = jnp.zeros_like(l_ref)
        acc_ref[...] = jnp.zeros_like(acc_ref)

    def chunk_scores(slot):
        kt = jnp.concatenate([kbuf[slot, j].reshape(dq, page).astype(BF16)
                              for j in range(pages)], axis=1)
        return _dot(qbd_ref[...], kt)

    def chunk_values(slot):
        return lambda h: jnp.concatenate(
            [vbuf[slot, j, pl.ds(h, page, stride=H_A), :].astype(BF16) for j in range(pages)],
            axis=0)

    def end_seq(k_new, v_new, lam, g, lam0):
        pad = jnp.zeros((LANES - t, dq), F32)
        kn = jnp.concatenate([k_new, pad], axis=0).astype(BF16)
        vn = jnp.concatenate([v_new, pad], axis=0).astype(BF16)
        s = _dot_nt(qbd_ref[...], kn)
        key_t = lax.broadcasted_iota(jnp.int32, s.shape, 1)
        qry_t = lax.broadcasted_iota(jnp.int32, s.shape, 0) % t
        update(jnp.where(key_t <= qry_t, s, -jnp.inf),
               lambda h: vn[:, h * HEAD_W:(h + 1) * HEAD_W])
        acc = acc_ref[...]
        l = l_ref[...]
        return jnp.concatenate(
            [_head_out(acc[2 * h * t:(2 * h + 1) * t], l[2 * h * t:(2 * h + 1) * t],
                       acc[(2 * h + 1) * t:(2 * h + 2) * t], l[(2 * h + 1) * t:(2 * h + 2) * t],
                       lam, g, lam0) for h in range(H_A)], axis=1)

    return start, wait, begin_seq, chunk_scores, softmax_stats, pv_update, chunk_values, end_seq


def _mixer_out_kernel(x_ref, o_ref, c_ref, wo_ref, g_ref, wq_ref, x1_ref, qm_ref, *, scale):
    da = o_ref.shape[1]
    x1 = (x_ref[...] + _dot(o_ref[...].astype(BF16), wo_ref[0:da, :])
          + _dot(c_ref[...], wo_ref[da:, :]))
    x1_ref[...] = x1
    xn = _rms(x1, g_ref[...]).astype(BF16)
    qm_ref[...] = (_dot(xn, wq_ref[...]) * scale).astype(qm_ref.dtype)


def _mixer_out(x, o, c, w_out, g_cross, w_q, *, scale, act_dtype):
    n, d = x.shape
    tm = min(ROW_TILE, n)
    row = lambda i: (i, 0)
    fixed = lambda i: (0, 0)
    return pl.pallas_call(
        functools.partial(_mixer_out_kernel, scale=scale),
        grid=(n // tm,),
        in_specs=[pl.BlockSpec((tm, d), row), pl.BlockSpec((tm, o.shape[1]), row),
                  pl.BlockSpec((tm, c.shape[1]), row), pl.BlockSpec((d, d), fixed),
                  pl.BlockSpec((1, d), fixed), pl.BlockSpec((d, d), fixed)],
        out_specs=[pl.BlockSpec((tm, d), row)] * 2,
        out_shape=[jax.ShapeDtypeStruct((n, d), F32), jax.ShapeDtypeStruct((n, d), act_dtype)],
        compiler_params=_params(1),
        name="mixer_out",
    )(x, o, c, w_out, g_cross, w_q)


def _cross_attn_kernel(q_ref, mk_ref, mv_ref, o_ref, *, seqs):
    dh = q_ref.shape[1] // H_MEM
    tq = q_ref.shape[0] // seqs
    mem_rows = mk_ref.shape[0] // seqs

    def head(ref, i, h):
        if ref.shape[1] == q_ref.shape[1]:
            return ref[i * mem_rows:(i + 1) * mem_rows, h * dh:(h + 1) * dh].astype(BF16)
        chunks = dh // LANES
        step = chunks * H_MEM
        return jnp.concatenate(
            [ref[pl.ds(i * mem_rows + c * H_MEM + h, mem_rows // step, stride=step), :]
             for c in range(chunks)], axis=1).astype(BF16)

    pairs = [(i, h) for i in range(seqs) for h in range(H_MEM)]
    block = lambda i, h: (slice(i * tq, (i + 1) * tq), slice(h * dh, (h + 1) * dh))
    s = [_dot_nt(q_ref[block(i, h)].astype(BF16), head(mk_ref, i, h)) for i, h in pairs]
    p = [jnp.exp(x - jnp.max(x, axis=-1, keepdims=True)) for x in s]
    l = [jnp.sum(x, axis=-1, keepdims=True) for x in p]
    for (i, h), x, y in zip(pairs, p, l):
        o_ref[block(i, h)] = (_dot(x.astype(BF16), head(mv_ref, i, h)) / y).astype(o_ref.dtype)


def _cross_attn(qm, mk, mv, *, rows_per_seq):
    n, d = qm.shape
    n_seq = n // rows_per_seq
    mem_rows = mk.shape[0] // n_seq
    if rows_per_seq >= ROW_TILE:
        tm, seqs = ROW_TILE, 1
        tps = rows_per_seq // tm
        mem = pl.BlockSpec((mem_rows, mk.shape[1]), lambda i: (i // tps, 0))
    else:
        seqs = min(SHORT_SEQS_PER_STEP, n_seq)
        tm = seqs * rows_per_seq
        mem = pl.BlockSpec((seqs * mem_rows, mk.shape[1]), lambda i: (i, 0))
    assert n % tm == 0
    return pl.pallas_call(
        functools.partial(_cross_attn_kernel, seqs=seqs),
        grid=(n // tm,),
        in_specs=[pl.BlockSpec((tm, d), lambda i: (i, 0)), mem, mem],
        out_specs=pl.BlockSpec((tm, d), lambda i: (i, 0)),
        out_shape=jax.ShapeDtypeStruct((n, d), qm.dtype),
        compiler_params=_params(1),
        name="cross_attn",
    )(qm, mk, mv)


def _mlp_kernel(x_ref, oc_ref, wo_ref, g_ref, wu_ref, wd_ref, gf_ref, y_ref, *, ff_chunk):
    x2 = x_ref[...] + _dot(oc_ref[...].astype(BF16), wo_ref[...])
    xn = _rms(x2, g_ref[...]).astype(BF16)
    acc = x2
    for c in range(wu_ref.shape[1] // ff_chunk):
        sl = slice(c * ff_chunk, (c + 1) * ff_chunk)
        h = jnp.maximum(_dot(xn, wu_ref[:, sl]), 0.0)
        acc = acc + _dot((h * h).astype(BF16), wd_ref[sl, :])
    y_ref[...] = _rms(acc, gf_ref[...])


def _mlp(x1, oc, w_o, g_mlp, w_up, w_down, g_final):
    n, d = x1.shape
    dff = w_up.shape[1]
    tm = min(ROW_TILE, n)
    row = lambda i: (i, 0)
    fixed = lambda i: (0, 0)
    return pl.pallas_call(
        functools.partial(_mlp_kernel, ff_chunk=min(1024, dff)),
        grid=(n // tm,),
        in_specs=[pl.BlockSpec((tm, d), row), pl.BlockSpec((tm, d), row),
                  pl.BlockSpec((d, d), fixed), pl.BlockSpec((1, d), fixed),
                  pl.BlockSpec((d, dff), fixed), pl.BlockSpec((dff, d), fixed),
                  pl.BlockSpec((1, d), fixed)],
        out_specs=pl.BlockSpec((tm, d), row),
        out_shape=jax.ShapeDtypeStruct((n, d), F32),
        compiler_params=_params(1),
        name="mlp",
    )(x1, oc, w_o, g_mlp, w_up, w_down, g_final)


def _mlp_paged_kernel(pt_ref, x_ref, oc_ref, wo_ref, g_ref, wu_ref, wd_ref, gf_ref,
                      q_ref, kn_ref, vn_ref, lq1, lk1, lq2, lk2, gs_ref, kpool, vpool,
                      y_ref, o_ref, kbuf, vbuf, sem, qbd_ref, m_ref, l_ref, acc_ref, *,
                      ff_chunk, pages, chunks_per_seq, lam0):
    i = pl.program_id(0)
    t = q_ref.shape[0]
    (start, wait, begin_seq, chunk_scores, softmax_stats, pv_update, chunk_values,
     end_seq) = _paged_ops(pt_ref, kpool, vpool, kbuf, vbuf, sem, qbd_ref, m_ref, l_ref, acc_ref,
                           pages=pages, t=t)
    g0 = i * chunks_per_seq

    @pl.when(i == 0)
    def _():
        start(0, 0)

    x2 = x_ref[...] + _dot(oc_ref[...].astype(BF16), wo_ref[...])
    xn = _rms(x2, g_ref[...]).astype(BF16)
    acc = x2
    n_ff = wu_ref.shape[1] // ff_chunk
    ff_done = 0
    begin_seq(q_ref[...])

    def up(sl):
        h = jnp.maximum(_dot(xn, wu_ref[:, sl]), 0.0)
        return (h * h).astype(BF16)

    for c in range(chunks_per_seq):
        if c + 1 < chunks_per_seq:
            start(g0 + c + 1, (c + 1) % 2)
        else:
            @pl.when(i + 1 < pl.num_programs(0))
            def _():
                start(g0 + chunks_per_seq, 0)
        wait(g0 + c, c % 2)
        sc = chunk_scores(c % 2)
        todo = []
        while ff_done < (c + 1) * n_ff // chunks_per_seq:
            todo.append(slice(ff_done * ff_chunk, (ff_done + 1) * ff_chunk))
            ff_done += 1
        hs = [(up(sl), sl) for sl in todo]
        pb, alpha = softmax_stats(sc)
        for hb, sl in hs:
            acc = acc + _dot(hb, wd_ref[sl, :])
        pv_update(pb, alpha, chunk_values(c % 2))
    lam = _diff_lambda(lq1, lk1, lq2, lk2, lam0)
    o_ref[...] = end_seq(kn_ref[...], vn_ref[...], lam, gs_ref[...], lam0).astype(o_ref.dtype)
    y_ref[...] = _rms(acc, gf_ref[...])


def _mlp_paged(x1, oc, w_o, g_mlp, w_up, w_down, g_final,
               q, k_new, v_new, lams, g_subln, pool_kt, pool_v, page_table, *, lam0):
    n, d = x1.shape
    dff = w_up.shape[1]
    tm = min(ROW_TILE, n)
    n_seq, n_pages = page_table.shape
    t = q.shape[0] // n_seq
    page = pool_kt.shape[3]
    pages = min(PAGES_PER_STEP, n_pages)
    cps = n_pages // pages
    assert n_seq == n // tm, "one sample sequence per MLP row tile"
    assert n_pages % pages == 0 and cps % 2 == 0
    row = lambda i, pt: (i, 0)
    fixed = lambda i, pt: (0, 0)
    once = dict(pipeline_mode=pl.Buffered(1))
    seq = pl.BlockSpec((t, D_ATTN), row)
    vec = pl.BlockSpec((1, DH), fixed)
    rows = N_SUB * t
    grid_spec = pltpu.PrefetchScalarGridSpec(
        num_scalar_prefetch=1,
        grid=(n // tm,),
        in_specs=[pl.BlockSpec((tm, d), row), pl.BlockSpec((tm, d), row),
                  pl.BlockSpec((d, d), fixed, **once), pl.BlockSpec((1, d), fixed),
                  pl.BlockSpec((d, dff), fixed, **once), pl.BlockSpec((dff, d), fixed, **once),
                  pl.BlockSpec((1, d), fixed),
                  seq, seq, seq, vec, vec, vec, vec, pl.BlockSpec((1, HEAD_W), fixed),
                  pl.BlockSpec(memory_space=pl.ANY), pl.BlockSpec(memory_space=pl.ANY)],
        out_specs=[pl.BlockSpec((tm, d), row), seq],
        scratch_shapes=[pltpu.VMEM((2, pages, N_SUB, DH, page), F32),
                        pltpu.VMEM((2, pages, page * H_A, HEAD_W), F32),
                        pltpu.SemaphoreType.DMA((2, 2, pages)),
                        pltpu.VMEM((rows, D_ATTN), BF16),
                        pltpu.VMEM((rows, 1), F32), pltpu.VMEM((rows, 1), F32),
                        pltpu.VMEM((rows, HEAD_W), F32)],
    )
    return pl.pallas_call(
        functools.partial(_mlp_paged_kernel, ff_chunk=min(FUSED_FF_CHUNK, dff), pages=pages,
                          chunks_per_seq=cps, lam0=lam0),
        grid_spec=grid_spec,
        out_shape=[jax.ShapeDtypeStruct((n, d), F32),
                   jax.ShapeDtypeStruct((n_seq * t, D_ATTN), F32)],
        compiler_params=_params(1),
        name="mlp_paged",
    )(page_table.reshape(-1), x1, oc, w_o, g_mlp, w_up, w_down, g_final,
      q, k_new, v_new, *lams, g_subln, pool_kt, pool_v)


def _rope_angles(pos):
    half = DH // 2
    inv = jnp.exp(jnp.arange(half, dtype=F32) * (-2.0 * math.log(ROPE_THETA) / DH))
    ang = pos.astype(F32)[:, None] * inv[None, :]
    return jnp.cos(ang), jnp.sin(ang)


def _rope_row_tables(pos, reps):
    cos, sin = _rope_angles(pos)
    cos = jnp.tile(jnp.concatenate([cos, cos], axis=-1), (reps, LANES // DH))
    sin = jnp.tile(jnp.concatenate([-sin, sin], axis=-1), (reps, LANES // DH))
    return cos, sin


def kernel(x_prompt, x_sample, mem_prompt, cache_k, cache_v, state_conv, cache_mem_k, cache_mem_v, page_table, g_mix, w_in, lambda_q1, lambda_k1, lambda_q2, lambda_k2, g_subln, conv_w, g_conv, w_out, g_cross, g_mem, w_q_mem, w_k_mem, w_v_mem, w_o_mem, g_mlp, w_up, w_down, g_final):
    assert w_in.shape[0] == 1, "single-layer trunk: the final RMSNorm is fused into the MLP kernel"
    b_p, s_p, d = x_prompt.shape
    b_s, t_s, _ = x_sample.shape
    n_mem = mem_prompt.shape[1]
    dc = conv_w.shape[-1]
    dh_mem = d // H_MEM
    mem_scale = dh_mem ** -0.5
    past_len = page_table.shape[1] * cache_k.shape[2]
    assert t_s >= CONV_W - 1 and (b_s * t_s) % SUBLANES == 0

    pos_p = jnp.arange(s_p)
    tabs_p = tuple(a.T for a in _rope_angles(pos_p))
    tm_s = min(ROW_TILE, b_s * t_s)
    tabs_s = _rope_row_tables(past_len + jnp.arange(t_s), tm_s // t_s)

    xp = x_prompt.reshape(b_p * s_p, d)
    xs = x_sample.reshape(b_s * t_s, d)
    mem = mem_prompt.reshape(b_p * n_mem, d)
    row = lambda a: a.reshape(1, -1)
    l = 0
    lam0 = _lambda_init(l)
    lams = [row(a[l]) for a in (lambda_q1, lambda_k1, lambda_q2, lambda_k2)]
    w_in_b = w_in[l].astype(BF16)
    w_qk_t = w_in[l][:, 0:2 * D_ATTN].T.astype(BF16)
    w_out_b = w_out[l].astype(BF16)
    w_q_b = w_q_mem[l].astype(BF16)
    w_o_b = w_o_mem[l].astype(BF16)
    w_up_b = w_up[l].astype(BF16)
    w_down_b = w_down[l].astype(BF16)

    def pre_mlp(x, o, c, mk, mv, rows_per_seq, act_dtype):
        x1, qm = _mixer_out(x, o, c, w_out_b, row(g_cross[l]), w_q_b, scale=mem_scale,
                            act_dtype=act_dtype)
        return x1, _cross_attn(qm, mk, mv, rows_per_seq=rows_per_seq)

    mlp_weights = (w_o_b, row(g_mlp[l]), w_up_b, w_down_b, row(g_final))

    mk, mv, mkb, mvb = _mem_kv(mem, row(g_mem[l]), w_k_mem[l].astype(BF16),
                               w_v_mem[l].astype(BF16))
    qt, kt, kb, vp, vt, c, conv_p = _mixer_in_prompt(
        xp, row(g_mix[l]), w_in_b, w_qk_t, tabs_p, conv_w[l], row(g_conv[l]), batch=b_p, seq=s_p)
    o = _prompt_attn(qt, kb, vt, lams, row(g_subln[l]), lam0=lam0)
    x1p, ocp = pre_mlp(xp, o, c, mkb, mvb, s_p, BF16)
    kp = kt.reshape(b_p, N_SUB, DH, s_p).transpose(0, 3, 1, 2)

    prev = state_conv[l]
    p1 = jnp.pad(prev[:, 1:2], ((0, 0), (0, t_s - 1), (0, 0))).reshape(b_s * t_s, dc)
    p2 = jnp.pad(prev, ((0, 0), (0, t_s - 2), (0, 0))).reshape(b_s * t_s, dc)
    q, ks, vs, c, u = _mixer_in_sample(xs, row(g_mix[l]), w_in_b, tabs_s, conv_w[l],
                                       row(g_conv[l]), (p1, p2), seq_rows=t_s)

    yp, o = _mlp_paged(x1p, ocp, *mlp_weights, q, ks, vs, lams, row(g_subln[l]),
                       cache_k[l].transpose(0, 2, 3, 1),
                       cache_v[l].reshape(cache_v.shape[1], -1, HEAD_W), page_table, lam0=lam0)

    def lane_rows(m):
        m = m.reshape(b_s * n_mem, H_MEM, dh_mem // LANES, LANES)
        return m.transpose(0, 2, 1, 3).reshape(-1, LANES)

    x1s, ocs = pre_mlp(xs, o, c, lane_rows(cache_mem_k[l]), lane_rows(cache_mem_v[l]), t_s, F32)
    ys = _mlp(x1s, ocs, *mlp_weights)
    conv_s = u.reshape(b_s, t_s, dc)[:, t_s - (CONV_W - 1):]

    return (yp.reshape(b_p, s_p, d), ys.reshape(b_s, t_s, d),
            kp[None], vp.reshape(1, b_p, s_p, H_A, HEAD_W), conv_p[None],
            mk.reshape(1, b_p, n_mem, H_MEM, dh_mem), mv.reshape(1, b_p, n_mem, H_MEM, dh_mem),
            ks.reshape(1, b_s, t_s, N_SUB, DH), vs.reshape(1, b_s, t_s, H_A, HEAD_W),
            conv_s[None])
```

```python
import functools
import math

import jax
import jax.numpy as jnp
from jax import lax
from jax.experimental import pallas as pl
from jax.experimental.pallas import tpu as pltpu

F32 = jnp.float32
BF16 = jnp.bfloat16

H_A = 4
DH = 64
N_SUB = 2 * H_A
HEAD_W = 2 * DH
D_ATTN = H_A * HEAD_W
CONV_W = 3
H_MEM = 4
ROPE_THETA = 10000.0
EPS = 1e-6
ATTN_SCALE = DH ** -0.5
LANES = 128
SUBLANES = 8
VMEM_LIMIT = 56 * 1024 * 1024

ONES_ROWS = 16
ROW_TILE = 512
SHORT_SEQS_PER_STEP = 4
FUSED_FF_CHUNK = 512
PAGES_PER_STEP = 8


def _lambda_init(l):
    return 0.8 - 0.6 * math.exp(-0.3 * l)


def _rms(x, g):
    return x * lax.rsqrt(jnp.mean(x * x, axis=-1, keepdims=True) + EPS) * g


def _dot(a, b):
    return jnp.dot(a, b, preferred_element_type=F32)


def _dot_nt(a, b):
    return lax.dot_general(a, b, (((1,), (1,)), ((), ())), preferred_element_type=F32)


def _params(n_axes):
    return pltpu.CompilerParams(dimension_semantics=("arbitrary",) * n_axes,
                                vmem_limit_bytes=VMEM_LIMIT)


def _diff_lambda(lq1_ref, lk1_ref, lq2_ref, lk2_ref, lam0):
    a = jnp.sum(lq1_ref[...] * lk1_ref[...], axis=-1, keepdims=True)
    b = jnp.sum(lq2_ref[...] * lk2_ref[...], axis=-1, keepdims=True)
    return jnp.exp(a) - jnp.exp(b) + lam0


def _store_heads(ref, x):
    w = ref.shape[2]
    for h in range(ref.shape[1]):
        ref[:, h, :] = x[:, h * w:(h + 1) * w].astype(ref.dtype)


def _mem_kv_kernel(m_ref, g_ref, wk_ref, wv_ref, k_ref, v_ref, kb_ref, vb_ref):
    mn = _rms(m_ref[...], g_ref[...]).astype(BF16)
    k = _dot(mn, wk_ref[...])
    v = _dot(mn, wv_ref[...])
    _store_heads(k_ref, k)
    _store_heads(v_ref, v)
    kb_ref[...] = k.astype(BF16)
    vb_ref[...] = v.astype(BF16)


def _mem_kv(mem, g, wk, wv):
    n, d = mem.shape
    tm = min(ROW_TILE, n)
    row = lambda i: (i, 0)
    fixed = lambda i: (0, 0)
    heads = pl.BlockSpec((tm, H_MEM, d // H_MEM), lambda i: (i, 0, 0))
    heads_shape = jax.ShapeDtypeStruct((n, H_MEM, d // H_MEM), F32)
    return pl.pallas_call(
        _mem_kv_kernel,
        grid=(n // tm,),
        in_specs=[pl.BlockSpec((tm, d), row), pl.BlockSpec((1, d), fixed),
                  pl.BlockSpec((d, d), fixed), pl.BlockSpec((d, d), fixed)],
        out_specs=[heads, heads, pl.BlockSpec((tm, d), row), pl.BlockSpec((tm, d), row)],
        out_shape=[heads_shape, heads_shape] + [jax.ShapeDtypeStruct((n, d), BF16)] * 2,
        compiler_params=_params(1),
        name="mem_kv",
    )(mem, g, wk, wv)


def _rope_rows(p, cos, sin):
    lane = lax.broadcasted_iota(jnp.int32, (p.shape[0], LANES), 1)
    first_half = (lane % DH) < (DH // 2)
    out = []
    for c in range(D_ATTN // LANES):
        pc = p[:, c * LANES:(c + 1) * LANES]
        swapped = jnp.where(first_half, pltpu.roll(pc, LANES - DH // 2, 1),
                            pltpu.roll(pc, DH // 2, 1))
        out.append(pc * cos + swapped * sin)
    return out


def _conv_branch(xn, w_ref, cw_ref, gc_ref, fix_history):
    dc = cw_ref.shape[1]
    o0 = 3 * D_ATTN
    gate_b = _dot(xn, w_ref[:, o0:o0 + dc])
    u = _dot(xn, w_ref[:, o0 + dc:o0 + 2 * dc]) * _dot(xn, w_ref[:, o0 + 2 * dc:o0 + 3 * dc])
    row = lax.broadcasted_iota(jnp.int32, u.shape, 0)
    um1, um2 = fix_history(row, pltpu.roll(u, 1, 0), pltpu.roll(u, 2, 0))
    y = cw_ref[0:1, :] * um2 + cw_ref[1:2, :] * um1 + cw_ref[2:3, :] * u
    return _rms(gate_b * y, gc_ref[...]).astype(BF16), u


def _rope_cols(pt, cos_t, sin_t, outs):
    half = DH // 2
    for s in range(N_SUB):
        x1 = pt[s * DH:s * DH + half, :]
        x2 = pt[s * DH + half:(s + 1) * DH, :]
        r1 = x1 * cos_t - x2 * sin_t
        r2 = x2 * cos_t + x1 * sin_t
        for ref, scale in outs:
            if scale != 1.0:
                r1, r2 = r1 * scale, r2 * scale
            ref[s * DH:s * DH + half, :] = r1.astype(ref.dtype)
            ref[s * DH + half:(s + 1) * DH, :] = r2.astype(ref.dtype)


def _mixer_in_prompt_kernel(x_ref, g_ref, w_ref, wt_ref, cost_ref, sint_ref, cw_ref, gc_ref,
                            qt_ref, kt_ref, kb_ref, v_ref, vt_ref, c_ref, cs_ref, carry_ref, *,
                            tiles_per_seq):
    tm = x_ref.shape[0]

    @pl.when(pl.program_id(0) % tiles_per_seq == 0)
    def _():
        carry_ref[...] = jnp.zeros_like(carry_ref)

    xn = _rms(x_ref[...], g_ref[...]).astype(BF16)

    def fix_history(row, um1, um2):
        prev0 = carry_ref[SUBLANES - 2:SUBLANES - 1, :]
        prev1 = carry_ref[SUBLANES - 1:SUBLANES, :]
        return (jnp.where(row == 0, prev1, um1),
                jnp.where(row == 0, prev0, jnp.where(row == 1, prev1, um2)))

    c, u = _conv_branch(xn, w_ref, cw_ref, gc_ref, fix_history)
    c_ref[...] = c
    carry_ref[...] = u[tm - SUBLANES:tm, :]
    cs_ref[...] = u[tm - (CONV_W - 1):tm, :]

    cos_t = cost_ref[...]
    sin_t = sint_ref[...]
    _rope_cols(_dot_nt(wt_ref[0:D_ATTN, :], xn), cos_t, sin_t, [(qt_ref, ATTN_SCALE)])
    _rope_cols(_dot_nt(wt_ref[D_ATTN:2 * D_ATTN, :], xn), cos_t, sin_t, [(kt_ref, 1.0)])
    kb_ref[...] = kt_ref[...].T.astype(kb_ref.dtype)
    v = _dot(xn, w_ref[:, 2 * D_ATTN:3 * D_ATTN])
    _store_heads(v_ref, v)
    vt_ref[...] = v.T.astype(vt_ref.dtype)


def _mixer_in_prompt(x, g, w_in, w_qk_t, tabs, conv_w, g_conv, *, batch, seq):
    n, d = x.shape
    dc = conv_w.shape[1]
    tm = min(ROW_TILE, seq)
    assert seq % tm == 0
    tps = seq // tm
    cos_t, sin_t = tabs
    row = lambda i: (i, 0)
    fixed = lambda i: (0, 0)
    tile_t = pl.BlockSpec((None, None, D_ATTN, tm), lambda i: (i // tps, i % tps, 0, 0))
    tile_t_shape = jax.ShapeDtypeStruct((batch, tps, D_ATTN, tm), BF16)
    return pl.pallas_call(
        functools.partial(_mixer_in_prompt_kernel, tiles_per_seq=tps),
        grid=(n // tm,),
        in_specs=[pl.BlockSpec((tm, d), row), pl.BlockSpec((1, d), fixed),
                  pl.BlockSpec(w_in.shape, fixed), pl.BlockSpec(w_qk_t.shape, fixed),
                  pl.BlockSpec((DH // 2, tm), lambda i: (0, i % tps)),
                  pl.BlockSpec((DH // 2, tm), lambda i: (0, i % tps)),
                  pl.BlockSpec((CONV_W, dc), fixed), pl.BlockSpec((1, dc), fixed)],
        out_specs=[tile_t,
                   pl.BlockSpec((None, D_ATTN, tm), lambda i: (i // tps, 0, i % tps)),
                   pl.BlockSpec((tm, D_ATTN), row),
                   pl.BlockSpec((tm, H_A, HEAD_W), lambda i: (i, 0, 0)),
                   tile_t,
                   pl.BlockSpec((tm, dc), row),
                   pl.BlockSpec((None, CONV_W - 1, dc), lambda i: (i // tps, 0, 0))],
        out_shape=[tile_t_shape,
                   jax.ShapeDtypeStruct((batch, D_ATTN, seq), F32),
                   jax.ShapeDtypeStruct((n, D_ATTN), BF16),
                   jax.ShapeDtypeStruct((n, H_A, HEAD_W), F32),
                   tile_t_shape,
                   jax.ShapeDtypeStruct((n, dc), BF16),
                   jax.ShapeDtypeStruct((batch, CONV_W - 1, dc), F32)],
        scratch_shapes=[pltpu.VMEM((SUBLANES, dc), F32)],
        compiler_params=_params(1),
        name="mixer_in_prompt",
    )(x, g, w_in, w_qk_t, cos_t, sin_t, conv_w, g_conv)


def _mixer_in_sample_kernel(x_ref, g_ref, w_ref, cos_ref, sin_ref, cw_ref, gc_ref, p1_ref, p2_ref,
                            q_ref, k_ref, v_ref, c_ref, u_ref, *, seq_rows):
    xn = _rms(x_ref[...], g_ref[...]).astype(BF16)
    cos = cos_ref[...]
    sin = sin_ref[...]
    for c, r in enumerate(_rope_rows(_dot(xn, w_ref[:, 0:D_ATTN]), cos, sin)):
        q_ref[:, c * LANES:(c + 1) * LANES] = r * ATTN_SCALE
    for c, r in enumerate(_rope_rows(_dot(xn, w_ref[:, D_ATTN:2 * D_ATTN]), cos, sin)):
        k_ref[:, c * LANES:(c + 1) * LANES] = r
    v_ref[...] = _dot(xn, w_ref[:, 2 * D_ATTN:3 * D_ATTN])

    def fix_history(row, um1, um2):
        t = row % seq_rows
        return jnp.where(t == 0, p1_ref[...], um1), jnp.where(t < 2, p2_ref[...], um2)

    c, u = _conv_branch(xn, w_ref, cw_ref, gc_ref, fix_history)
    c_ref[...] = c
    u_ref[...] = u


def _mixer_in_sample(x, g, w_in, tabs, conv_w, g_conv, hist, *, seq_rows):
    n, d = x.shape
    dc = conv_w.shape[1]
    tm = min(ROW_TILE, n)
    cos, sin = tabs
    row = lambda i: (i, 0)
    fixed = lambda i: (0, 0)
    act = pl.BlockSpec((tm, D_ATTN), row)
    conv = pl.BlockSpec((tm, dc), row)
    return pl.pallas_call(
        functools.partial(_mixer_in_sample_kernel, seq_rows=seq_rows),
        grid=(n // tm,),
        in_specs=[pl.BlockSpec((tm, d), row), pl.BlockSpec((1, d), fixed),
                  pl.BlockSpec(w_in.shape, fixed),
                  pl.BlockSpec((tm, LANES), fixed), pl.BlockSpec((tm, LANES), fixed),
                  pl.BlockSpec((CONV_W, dc), fixed), pl.BlockSpec((1, dc), fixed), conv, conv],
        out_specs=[act, act, act, conv, conv],
        out_shape=[jax.ShapeDtypeStruct((n, D_ATTN), F32)] * 3
                  + [jax.ShapeDtypeStruct((n, dc), BF16), jax.ShapeDtypeStruct((n, dc), F32)],
        compiler_params=_params(1),
        name="mixer_in_sample",
    )(x, g, w_in, cos, sin, conv_w, g_conv, *hist)


def _head_out(a0, l0, a1, l1, lam, g, lam0):
    o = a0 / l0 - lam * (a1 / l1)
    return _rms(o, g) * (1.0 - lam0)


def _prompt_attn_kernel(qt_ref, k_ref, vt_ref, lq1, lk1, lq2, lk2, gs_ref, o_ref,
                        qt2_ref, s_ref, vx_ref, m_ref, acc_ref, *, lam0):
    nk, _, tq = qt_ref.shape
    vx_ref[:, 0:HEAD_W, :] = vt_ref[...]
    vx_ref[:, HEAD_W:, :] = jnp.ones((nk, ONES_ROWS, tq), vx_ref.dtype)
    lam = _diff_lambda(lq1, lk1, lq2, lk2, lam0)

    def scores(qi, j, sub):
        k = k_ref[pl.ds(pl.multiple_of(j * tq, tq), tq), :]
        s_ref[sub] = _dot(k, qt2_ref[qi % 2, sub])

    def softmax_pv(qi, j, sub, masked):
        st = s_ref[sub]
        if masked:
            key = lax.broadcasted_iota(jnp.int32, st.shape, 0)
            qry = lax.broadcasted_iota(jnp.int32, st.shape, 1)
            st = jnp.where(key <= qry, st, -jnp.inf)
        m_prev = m_ref[qi % 2, sub]
        m_new = jnp.maximum(m_prev, jnp.max(st, axis=0, keepdims=True))
        p = jnp.exp(st - m_new)
        alpha = jnp.exp(m_prev - m_new)
        acc_ref[qi % 2, sub] = alpha * acc_ref[qi % 2, sub] + _dot(vx_ref[j], p.astype(BF16))
        m_ref[qi % 2, sub] = m_new

    def head(qi):
        qt = qt_ref[qi]
        d_row = lax.broadcasted_iota(jnp.int32, qt.shape, 0)
        qt2_ref[qi % 2, 0] = jnp.where(d_row < DH, qt, jnp.zeros_like(qt))
        qt2_ref[qi % 2, 1] = jnp.where(d_row >= DH, qt, jnp.zeros_like(qt))
        m_ref[qi % 2] = jnp.full(m_ref.shape[1:], -jnp.inf, F32)
        acc_ref[qi % 2] = jnp.zeros(acc_ref.shape[1:], F32)
        scores(qi, 0, 0)

    def finish(qi):
        a0 = acc_ref[qi % 2, 0]
        a1 = acc_ref[qi % 2, 1]
        ot = (a0[0:HEAD_W] / a0[HEAD_W:HEAD_W + 1]
              - lam * (a1[0:HEAD_W] / a1[HEAD_W:HEAD_W + 1]))
        o_ref[qi * tq:(qi + 1) * tq, :] = (_rms(ot.T, gs_ref[...])
                                           * (1.0 - lam0)).astype(o_ref.dtype)

    head(0)
    for qi in range(nk):
        def body(j, carry, qi=qi):
            scores(qi, j, 1)
            softmax_pv(qi, j, 0, False)
            scores(qi, j + 1, 0)
            softmax_pv(qi, j, 1, False)
            return carry

        lax.fori_loop(0, qi, body, 0)
        scores(qi, qi, 1)
        softmax_pv(qi, qi, 0, True)
        if qi + 1 < nk:
            head(qi + 1)
        softmax_pv(qi, qi, 1, True)
        finish(qi)


def _prompt_attn(qt, kb, vt, lams, g_subln, *, lam0):
    batch, nk, _, tk = qt.shape
    seq = nk * tk
    vec = pl.BlockSpec((1, DH), lambda b, h: (0, 0))
    tiles_t = pl.BlockSpec((None, nk, HEAD_W, tk), lambda b, h: (b, 0, h, 0))
    return pl.pallas_call(
        functools.partial(_prompt_attn_kernel, lam0=lam0),
        grid=(batch, H_A),
        in_specs=[tiles_t, pl.BlockSpec((seq, HEAD_W), lambda b, h: (b, h)), tiles_t,
                  vec, vec, vec, vec, pl.BlockSpec((1, HEAD_W), lambda b, h: (0, 0))],
        out_specs=pl.BlockSpec((seq, HEAD_W), lambda b, h: (b, h)),
        out_shape=jax.ShapeDtypeStruct((batch * seq, D_ATTN), BF16),
        scratch_shapes=[pltpu.VMEM((2, 2, HEAD_W, tk), BF16),
                        pltpu.VMEM((2, tk, tk), F32),
                        pltpu.VMEM((nk, HEAD_W + ONES_ROWS, tk), BF16),
                        pltpu.VMEM((2, 2, 1, tk), F32),
                        pltpu.VMEM((2, 2, HEAD_W + ONES_ROWS, tk), F32)],
        compiler_params=_params(2),
        name="prompt_attn",
    )(qt, kb, vt, *lams, g_subln)


def _paged_ops(pt_ref, kpool, vpool, kbuf, vbuf, sem, qbd_ref, m_ref, l_ref, acc_ref, *, pages, t):
    rows, dq = qbd_ref.shape
    page = kbuf.shape[-1]

    def page_copies(g, slot):
        out = []
        for j in range(pages):
            pid = pt_ref[g * pages + j]
            out.append(pltpu.make_async_copy(kpool.at[pid], kbuf.at[slot, j], sem.at[0, slot, j]))
            out.append(pltpu.make_async_copy(vpool.at[pid], vbuf.at[slot, j], sem.at[1, slot, j]))
        return out

    def start(g, slot):
        for cp in page_copies(g, slot):
            cp.start()

    def wait(g, slot):
        for cp in page_copies(g, slot):
            cp.wait()

    def softmax_stats(s):
        m_prev = m_ref[...]
        m_new = jnp.maximum(m_prev, jnp.max(s, axis=-1, keepdims=True))
        alpha = jnp.exp(m_prev - m_new)
        p = jnp.exp(s - m_new)
        l_ref[...] = alpha * l_ref[...] + jnp.sum(p, axis=-1, keepdims=True)
        m_ref[...] = m_new
        return p.astype(BF16), alpha

    def pv_update(pb, alpha, value):
        pv = []
        for h in range(0, H_A, 2):
            both = _dot(pb[2 * h * t:2 * (h + 2) * t, :],
                        jnp.concatenate([value(h), value(h + 1)], axis=1))
            pv += [both[0:2 * t, 0:HEAD_W], both[2 * t:4 * t, HEAD_W:]]
        acc_ref[...] = alpha * acc_ref[...] + jnp.concatenate(pv, axis=0)

    def update(s, value):
        pb, alpha = softmax_stats(s)
        pv_update(pb, alpha, value)

    def begin_seq(q):
        qt = jnp.concatenate([q] * N_SUB, axis=0)
        r_i = lax.broadcasted_iota(jnp.int32, (rows, dq), 0)
        c_i = lax.broadcasted_iota(jnp.int32, (rows, dq), 1)
        qbd_ref[...] = jnp.where((r_i // t) == (c_i // DH), qt, jnp.zeros_like(qt)).astype(BF16)
        m_ref[...] = jnp.full_like(m_ref, -jnp.inf)
        l_ref[...] = jnp.zeros_like(l_ref)
        acc_ref[...] = jnp.zeros_like(acc_ref)

    def chunk_scores(slot):
        kt = jnp.concatenate([kbuf[slot, j].reshape(dq, page).astype(BF16)
                              for j in range(pages)], axis=1)
        return _dot(qbd_ref[...], kt)

    def chunk_values(slot):
        return lambda h: jnp.concatenate(
            [vbuf[slot, j, pl.ds(h, page, stride=H_A), :].astype(BF16) for j in range(pages)],
            axis=0)

    def end_seq(k_new, v_new, lam, g, lam0):
        pad = jnp.zeros((LANES - t, dq), F32)
        kn = jnp.concatenate([k_new, pad], axis=0).astype(BF16)
        vn = jnp.concatenate([v_new, pad], axis=0).astype(BF16)
        s = _dot_nt(qbd_ref[...], kn)
        key_t = lax.broadcasted_iota(jnp.int32, s.shape, 1)
        qry_t = lax.broadcasted_iota(jnp.int32, s.shape, 0) % t
        update(jnp.where(key_t <= qry_t, s, -jnp.inf),
               lambda h: vn[:, h * HEAD_W:(h + 1) * HEAD_W])
        acc = acc_ref[...]
        l = l_ref[...]
        return jnp.concatenate(
            [_head_out(acc[2 * h * t:(2 * h + 1) * t], l[2 * h * t:(2 * h + 1) * t],
                       acc[(2 * h + 1) * t:(2 * h + 2) * t], l[(2 * h + 1) * t:(2 * h + 2) * t],
                       lam, g, lam0) for h in range(H_A)], axis=1)

    return start, wait, begin_seq, chunk_scores, softmax_stats, pv_update, chunk_values, end_seq


def _mixer_out_kernel(x_ref, o_ref, c_ref, wo_ref, g_ref, wq_ref, x1_ref, qm_ref, *, scale):
    da = o_ref.shape[1]
    x1 = (x_ref[...] + _dot(o_ref[...].astype(BF16), wo_ref[0:da, :])
          + _dot(c_ref[...], wo_ref[da:, :]))
    x1_ref[...] = x1
    xn = _rms(x1, g_ref[...]).astype(BF16)
    qm_ref[...] = (_dot(xn, wq_ref[...]) * scale).astype(qm_ref.dtype)


def _mixer_out(x, o, c, w_out, g_cross, w_q, *, scale, act_dtype):
    n, d = x.shape
    tm = min(ROW_TILE, n)
    row = lambda i: (i, 0)
    fixed = lambda i: (0, 0)
    return pl.pallas_call(
        functools.partial(_mixer_out_kernel, scale=scale),
        grid=(n // tm,),
        in_specs=[pl.BlockSpec((tm, d), row), pl.BlockSpec((tm, o.shape[1]), row),
                  pl.BlockSpec((tm, c.shape[1]), row), pl.BlockSpec((d, d), fixed),
                  pl.BlockSpec((1, d), fixed), pl.BlockSpec((d, d), fixed)],
        out_specs=[pl.BlockSpec((tm, d), row)] * 2,
        out_shape=[jax.ShapeDtypeStruct((n, d), F32), jax.ShapeDtypeStruct((n, d), act_dtype)],
        compiler_params=_params(1),
        name="mixer_out",
    )(x, o, c, w_out, g_cross, w_q)


def _cross_attn_kernel(q_ref, mk_ref, mv_ref, o_ref, *, seqs):
    dh = q_ref.shape[1] // H_MEM
    tq = q_ref.shape[0] // seqs
    mem_rows = mk_ref.shape[0] // seqs

    def head(ref, i, h):
        if ref.shape[1] == q_ref.shape[1]:
            return ref[i * mem_rows:(i + 1) * mem_rows, h * dh:(h + 1) * dh].astype(BF16)
        chunks = dh // LANES
        step = chunks * H_MEM
        return jnp.concatenate(
            [ref[pl.ds(i * mem_rows + c * H_MEM + h, mem_rows // step, stride=step), :]
             for c in range(chunks)], axis=1).astype(BF16)

    pairs = [(i, h) for i in range(seqs) for h in range(H_MEM)]
    block = lambda i, h: (slice(i * tq, (i + 1) * tq), slice(h * dh, (h + 1) * dh))
    s = [_dot_nt(q_ref[block(i, h)].astype(BF16), head(mk_ref, i, h)) for i, h in pairs]
    p = [jnp.exp(x - jnp.max(x, axis=-1, keepdims=True)) for x in s]
    l = [jnp.sum(x, axis=-1, keepdims=True) for x in p]
    for (i, h), x, y in zip(pairs, p, l):
        o_ref[block(i, h)] = (_dot(x.astype(BF16), head(mv_ref, i, h)) / y).astype(o_ref.dtype)


def _cross_attn(qm, mk, mv, *, rows_per_seq):
    n, d = qm.shape
    n_seq = n // rows_per_seq
    mem_rows = mk.shape[0] // n_seq
    if rows_per_seq >= ROW_TILE:
        tm, seqs = ROW_TILE, 1
        tps = rows_per_seq // tm
        mem = pl.BlockSpec((mem_rows, mk.shape[1]), lambda i: (i // tps, 0))
    else:
        seqs = min(SHORT_SEQS_PER_STEP, n_seq)
        tm = seqs * rows_per_seq
        mem = pl.BlockSpec((seqs * mem_rows, mk.shape[1]), lambda i: (i, 0))
    assert n % tm == 0
    return pl.pallas_call(
        functools.partial(_cross_attn_kernel, seqs=seqs),
        grid=(n // tm,),
        in_specs=[pl.BlockSpec((tm, d), lambda i: (i, 0)), mem, mem],
        out_specs=pl.BlockSpec((tm, d), lambda i: (i, 0)),
        out_shape=jax.ShapeDtypeStruct((n, d), qm.dtype),
        compiler_params=_params(1),
        name="cross_attn",
    )(qm, mk, mv)


def _mlp_kernel(x_ref, oc_ref, wo_ref, g_ref, wu_ref, wd_ref, gf_ref, y_ref, *, ff_chunk):
    x2 = x_ref[...] + _dot(oc_ref[...].astype(BF16), wo_ref[...])
    xn = _rms(x2, g_ref[...]).astype(BF16)
    acc = x2
    for c in range(wu_ref.shape[1] // ff_chunk):
        sl = slice(c * ff_chunk, (c + 1) * ff_chunk)
        h = jnp.maximum(_dot(xn, wu_ref[:, sl]), 0.0)
        acc = acc + _dot((h * h).astype(BF16), wd_ref[sl, :])
    y_ref[...] = _rms(acc, gf_ref[...])


def _mlp(x1, oc, w_o, g_mlp, w_up, w_down, g_final):
    n, d = x1.shape
    dff = w_up.shape[1]
    tm = min(ROW_TILE, n)
    row = lambda i: (i, 0)
    fixed = lambda i: (0, 0)
    return pl.pallas_call(
        functools.partial(_mlp_kernel, ff_chunk=min(1024, dff)),
        grid=(n // tm,),
        in_specs=[pl.BlockSpec((tm, d), row), pl.BlockSpec((tm, d), row),
                  pl.BlockSpec((d, d), fixed), pl.BlockSpec((1, d), fixed),
                  pl.BlockSpec((d, dff), fixed), pl.BlockSpec((dff, d), fixed),
                  pl.BlockSpec((1, d), fixed)],
        out_specs=pl.BlockSpec((tm, d), row),
        out_shape=jax.ShapeDtypeStruct((n, d), F32),
        compiler_params=_params(1),
        name="mlp",
    )(x1, oc, w_o, g_mlp, w_up, w_down, g_final)


def _mlp_paged_kernel(pt_ref, x_ref, oc_ref, wo_ref, g_ref, wu_ref, wd_ref, gf_ref,
                      q_ref, kn_ref, vn_ref, lq1, lk1, lq2, lk2, gs_ref, kpool, vpool,
                      y_ref, o_ref, kbuf, vbuf, sem, qbd_ref, m_ref, l_ref, acc_ref, *,
                      ff_chunk, pages, chunks_per_seq, lam0):
    i = pl.program_id(0)
    t = q_ref.shape[0]
    (start, wait, begin_seq, chunk_scores, softmax_stats, pv_update, chunk_values,
     end_seq) = _paged_ops(pt_ref, kpool, vpool, kbuf, vbuf, sem, qbd_ref, m_ref, l_ref, acc_ref,
                           pages=pages, t=t)
    g0 = i * chunks_per_seq

    @pl.when(i == 0)
    def _():
        start(0, 0)

    x2 = x_ref[...] + _dot(oc_ref[...].astype(BF16), wo_ref[...])
    xn = _rms(x2, g_ref[...]).astype(BF16)
    acc = x2
    n_ff = wu_ref.shape[1] // ff_chunk
    ff_done = 0
    begin_seq(q_ref[...])

    def up(sl):
        h = jnp.maximum(_dot(xn, wu_ref[:, sl]), 0.0)
        return (h * h).astype(BF16)

    for c in range(chunks_per_seq):
        if c + 1 < chunks_per_seq:
            start(g0 + c + 1, (c + 1) % 2)
        else:
            @pl.when(i + 1 < pl.num_programs(0))
            def _():
                start(g0 + chunks_per_seq, 0)
        wait(g0 + c, c % 2)
        sc = chunk_scores(c % 2)
        todo = []
        while ff_done < (c + 1) * n_ff // chunks_per_seq:
            todo.append(slice(ff_done * ff_chunk, (ff_done + 1) * ff_chunk))
            ff_done += 1
        hs = [(up(sl), sl) for sl in todo]
        pb, alpha = softmax_stats(sc)
        for hb, sl in hs:
            acc = acc + _dot(hb, wd_ref[sl, :])
        pv_update(pb, alpha, chunk_values(c % 2))
    lam = _diff_lambda(lq1, lk1, lq2, lk2, lam0)
    o_ref[...] = end_seq(kn_ref[...], vn_ref[...], lam, gs_ref[...], lam0).astype(o_ref.dtype)
    y_ref[...] = _rms(acc, gf_ref[...])


def _mlp_paged(x1, oc, w_o, g_mlp, w_up, w_down, g_final,
               q, k_new, v_new, lams, g_subln, pool_kt, pool_v, page_table, *, lam0):
    n, d = x1.shape
    dff = w_up.shape[1]
    tm = min(ROW_TILE, n)
    n_seq, n_pages = page_table.shape
    t = q.shape[0] // n_seq
    page = pool_kt.shape[3]
    pages = min(PAGES_PER_STEP, n_pages)
    cps = n_pages // pages
    assert n_seq == n // tm, "one sample sequence per MLP row tile"
    assert n_pages % pages == 0 and cps % 2 == 0
    row = lambda i, pt: (i, 0)
    fixed = lambda i, pt: (0, 0)
    once = dict(pipeline_mode=pl.Buffered(1))
    seq = pl.BlockSpec((t, D_ATTN), row)
    vec = pl.BlockSpec((1, DH), fixed)
    rows = N_SUB * t
    grid_spec = pltpu.PrefetchScalarGridSpec(
        num_scalar_prefetch=1,
        grid=(n // tm,),
        in_specs=[pl.BlockSpec((tm, d), row), pl.BlockSpec((tm, d), row),
                  pl.BlockSpec((d, d), fixed, **once), pl.BlockSpec((1, d), fixed),
                  pl.BlockSpec((d, dff), fixed, **once), pl.BlockSpec((dff, d), fixed, **once),
                  pl.BlockSpec((1, d), fixed),
                  seq, seq, seq, vec, vec, vec, vec, pl.BlockSpec((1, HEAD_W), fixed),
                  pl.BlockSpec(memory_space=pl.ANY), pl.BlockSpec(memory_space=pl.ANY)],
        out_specs=[pl.BlockSpec((tm, d), row), seq],
        scratch_shapes=[pltpu.VMEM((2, pages, N_SUB, DH, page), F32),
                        pltpu.VMEM((2, pages, page * H_A, HEAD_W), F32),
                        pltpu.SemaphoreType.DMA((2, 2, pages)),
                        pltpu.VMEM((rows, D_ATTN), BF16),
                        pltpu.VMEM((rows, 1), F32), pltpu.VMEM((rows, 1), F32),
                        pltpu.VMEM((rows, HEAD_W), F32)],
    )
    return pl.pallas_call(
        functools.partial(_mlp_paged_kernel, ff_chunk=min(FUSED_FF_CHUNK, dff), pages=pages,
                          chunks_per_seq=cps, lam0=lam0),
        grid_spec=grid_spec,
        out_shape=[jax.ShapeDtypeStruct((n, d), F32),
                   jax.ShapeDtypeStruct((n_seq * t, D_ATTN), F32)],
        compiler_params=_params(1),
        name="mlp_paged",
    )(page_table.reshape(-1), x1, oc, w_o, g_mlp, w_up, w_down, g_final,
      q, k_new, v_new, *lams, g_subln, pool_kt, pool_v)


def _rope_angles(pos):
    half = DH // 2
    inv = jnp.exp(jnp.arange(half, dtype=F32) * (-2.0 * math.log(ROPE_THETA) / DH))
    ang = pos.astype(F32)[:, None] * inv[None, :]
    return jnp.cos(ang), jnp.sin(ang)


def _rope_row_tables(pos, reps):
    cos, sin = _rope_angles(pos)
    cos = jnp.tile(jnp.concatenate([cos, cos], axis=-1), (reps, LANES // DH))
    sin = jnp.tile(jnp.concatenate([-sin, sin], axis=-1), (reps, LANES // DH))
    return cos, sin


def kernel(x_prompt, x_sample, mem_prompt, cache_k, cache_v, state_conv, cache_mem_k, cache_mem_v, page_table, g_mix, w_in, lambda_q1, lambda_k1, lambda_q2, lambda_k2, g_subln, conv_w, g_conv, w_out, g_cross, g_mem, w_q_mem, w_k_mem, w_v_mem, w_o_mem, g_mlp, w_up, w_down, g_final):
    assert w_in.shape[0] == 1, "single-layer trunk: the final RMSNorm is fused into the MLP kernel"
    b_p, s_p, d = x_prompt.shape
    b_s, t_s, _ = x_sample.shape
    n_mem = mem_prompt.shape[1]
    dc = conv_w.shape[-1]
    dh_mem = d // H_MEM
    mem_scale = dh_mem ** -0.5
    past_len = page_table.shape[1] * cache_k.shape[2]
    assert t_s >= CONV_W - 1 and (b_s * t_s) % SUBLANES == 0

    pos_p = jnp.arange(s_p)
    tabs_p = tuple(a.T for a in _rope_angles(pos_p))
    tm_s = min(ROW_TILE, b_s * t_s)
    tabs_s = _rope_row_tables(past_len + jnp.arange(t_s), tm_s // t_s)

    xp = x_prompt.reshape(b_p * s_p, d)
    xs = x_sample.reshape(b_s * t_s, d)
    mem = mem_prompt.reshape(b_p * n_mem, d)
    row = lambda a: a.reshape(1, -1)
    l = 0
    lam0 = _lambda_init(l)
    lams = [row(a[l]) for a in (lambda_q1, lambda_k1, lambda_q2, lambda_k2)]
    w_in_b = w_in[l].astype(BF16)
    w_qk_t = w_in[l][:, 0:2 * D_ATTN].T.astype(BF16)
    w_out_b = w_out[l].astype(BF16)
    w_q_b = w_q_mem[l].astype(BF16)
    w_o_b = w_o_mem[l].astype(BF16)
    w_up_b = w_up[l].astype(BF16)
    w_down_b = w_down[l].astype(BF16)

    def pre_mlp(x, o, c, mk, mv, rows_per_seq, act_dtype):
        x1, qm = _mixer_out(x, o, c, w_out_b, row(g_cross[l]), w_q_b, scale=mem_scale,
                            act_dtype=act_dtype)
        return x1, _cross_attn(qm, mk, mv, rows_per_seq=rows_per_seq)

    mlp_weights = (w_o_b, row(g_mlp[l]), w_up_b, w_down_b, row(g_final))

    mk, mv, mkb, mvb = _mem_kv(mem, row(g_mem[l]), w_k_mem[l].astype(BF16),
                               w_v_mem[l].astype(BF16))
    qt, kt, kb, vp, vt, c, conv_p = _mixer_in_prompt(
        xp, row(g_mix[l]), w_in_b, w_qk_t, tabs_p, conv_w[l], row(g_conv[l]), batch=b_p, seq=s_p)
    o = _prompt_attn(qt, kb, vt, lams, row(g_subln[l]), lam0=lam0)
    x1p, ocp = pre_mlp(xp, o, c, mkb, mvb, s_p, BF16)
    kp = kt.reshape(b_p, N_SUB, DH, s_p).transpose(0, 3, 1, 2)

    prev = state_conv[l]
    p1 = jnp.pad(prev[:, 1:2], ((0, 0), (0, t_s - 1), (0, 0))).reshape(b_s * t_s, dc)
    p2 = jnp.pad(prev, ((0, 0), (0, t_s - 2), (0, 0))).reshape(b_s * t_s, dc)
    q, ks, vs, c, u = _mixer_in_sample(xs, row(g_mix[l]), w_in_b, tabs_s, conv_w[l],
                                       row(g_conv[l]), (p1, p2), seq_rows=t_s)

    yp, o = _mlp_paged(x1p, ocp, *mlp_weights, q, ks, vs, lams, row(g_subln[l]),
                       cache_k[l].transpose(0, 2, 3, 1),
                       cache_v[l].reshape(cache_v.shape[1], -1, HEAD_W), page_table, lam0=lam0)

    def lane_rows(m):
        m = m.reshape(b_s * n_mem, H_MEM, dh_mem // LANES, LANES)
        return m.transpose(0, 2, 1, 3).reshape(-1, LANES)

    x1s, ocs = pre_mlp(xs, o, c, lane_rows(cache_mem_k[l]), lane_rows(cache_mem_v[l]), t_s, F32)
    ys = _mlp(x1s, ocs, *mlp_weights)
    conv_s = u.reshape(b_s, t_s, dc)[:, t_s - (CONV_W - 1):]

    return (yp.reshape(b_p, s_p, d), ys.reshape(b_s, t_s, d),
            kp[None], vp.reshape(1, b_p, s_p, H_A, HEAD_W), conv_p[None],
            mk.reshape(1, b_p, n_mem, H_MEM, dh_mem), mv.reshape(1, b_p, n_mem, H_MEM, dh_mem),
            ks.reshape(1, b_s, t_s, N_SUB, DH), vs.reshape(1, b_s, t_s, H_A, HEAD_W),
            conv_s[None])
```

```python
import functools
import math

import jax
import jax.numpy as jnp
from jax import lax
from jax.experimental import pallas as pl
from jax.experimental.pallas import tpu as pltpu

F32 = jnp.float32
BF16 = jnp.bfloat16

H_A = 4
DH = 64
N_SUB = 2 * H_A
HEAD_W = 2 * DH
D_ATTN = H_A * HEAD_W
CONV_W = 3
H_MEM = 4
ROPE_THETA = 10000.0
EPS = 1e-6
ATTN_SCALE = DH ** -0.5
LANES = 128
SUBLANES = 8
VMEM_LIMIT = 56 * 1024 * 1024

ONES_ROWS = 16
ROW_TILE = 512
SHORT_SEQS_PER_STEP = 4
FUSED_FF_CHUNK = 512
PAGES_PER_STEP = 8


def _lambda_init(l):
    return 0.8 - 0.6 * math.exp(-0.3 * l)


def _rms(x, g):
    return x * lax.rsqrt(jnp.mean(x * x, axis=-1, keepdims=True) + EPS) * g


def _dot(a, b):
    return jnp.dot(a, b, preferred_element_type=F32)


def _dot_nt(a, b):
    return lax.dot_general(a, b, (((1,), (1,)), ((), ())), preferred_element_type=F32)


def _params(n_axes):
    return pltpu.CompilerParams(dimension_semantics=("arbitrary",) * n_axes,
                                vmem_limit_bytes=VMEM_LIMIT)


def _diff_lambda(lq1_ref, lk1_ref, lq2_ref, lk2_ref, lam0):
    a = jnp.sum(lq1_ref[...] * lk1_ref[...], axis=-1, keepdims=True)
    b = jnp.sum(lq2_ref[...] * lk2_ref[...], axis=-1, keepdims=True)
    return jnp.exp(a) - jnp.exp(b) + lam0


def _store_heads(ref, x):
    w = ref.shape[2]
    for h in range(ref.shape[1]):
        ref[:, h, :] = x[:, h * w:(h + 1) * w].astype(ref.dtype)


def _mem_kv_kernel(m_ref, g_ref, wk_ref, wv_ref, k_ref, v_ref, kb_ref, vb_ref):
    mn = _rms(m_ref[...], g_ref[...]).astype(BF16)
    k = _dot(mn, wk_ref[...])
    v = _dot(mn, wv_ref[...])
    _store_heads(k_ref, k)
    _store_heads(v_ref, v)
    kb_ref[...] = k.astype(BF16)
    vb_ref[...] = v.astype(BF16)


def _mem_kv(mem, g, wk, wv):
    n, d = mem.shape
    tm = min(ROW_TILE, n)
    row = lambda i: (i, 0)
    fixed = lambda i: (0, 0)
    heads = pl.BlockSpec((tm, H_MEM, d // H_MEM), lambda i: (i, 0, 0))
    heads_shape = jax.ShapeDtypeStruct((n, H_MEM, d // H_MEM), F32)
    return pl.pallas_call(
        _mem_kv_kernel,
        grid=(n // tm,),
        in_specs=[pl.BlockSpec((tm, d), row), pl.BlockSpec((1, d), fixed),
                  pl.BlockSpec((d, d), fixed), pl.BlockSpec((d, d), fixed)],
        out_specs=[heads, heads, pl.BlockSpec((tm, d), row), pl.BlockSpec((tm, d), row)],
        out_shape=[heads_shape, heads_shape] + [jax.ShapeDtypeStruct((n, d), BF16)] * 2,
        compiler_params=_params(1),
        name="mem_kv",
    )(mem, g, wk, wv)


def _rope_rows(p, cos, sin):
    lane = lax.broadcasted_iota(jnp.int32, (p.shape[0], LANES), 1)
    first_half = (lane % DH) < (DH // 2)
    out = []
    for c in range(D_ATTN // LANES):
        pc = p[:, c * LANES:(c + 1) * LANES]
        swapped = jnp.where(first_half, pltpu.roll(pc, LANES - DH // 2, 1),
                            pltpu.roll(pc, DH // 2, 1))
        out.append(pc * cos + swapped * sin)
    return out


def _conv_branch(xn, w_ref, cw_ref, gc_ref, fix_history):
    dc = cw_ref.shape[1]
    o0 = 3 * D_ATTN
    gate_b = _dot(xn, w_ref[:, o0:o0 + dc])
    u = _dot(xn, w_ref[:, o0 + dc:o0 + 2 * dc]) * _dot(xn, w_ref[:, o0 + 2 * dc:o0 + 3 * dc])
    row = lax.broadcasted_iota(jnp.int32, u.shape, 0)
    um1, um2 = fix_history(row, pltpu.roll(u, 1, 0), pltpu.roll(u, 2, 0))
    y = cw_ref[0:1, :] * um2 + cw_ref[1:2, :] * um1 + cw_ref[2:3, :] * u
    return _rms(gate_b * y, gc_ref[...]).astype(BF16), u


def _rope_cols(pt, cos_t, sin_t, outs):
    half = DH // 2
    for s in range(N_SUB):
        x1 = pt[s * DH:s * DH + half, :]
        x2 = pt[s * DH + half:(s + 1) * DH, :]
        r1 = x1 * cos_t - x2 * sin_t
        r2 = x2 * cos_t + x1 * sin_t
        for ref, scale in outs:
            if scale != 1.0:
                r1, r2 = r1 * scale, r2 * scale
            ref[s * DH:s * DH + half, :] = r1.astype(ref.dtype)
            ref[s * DH + half:(s + 1) * DH, :] = r2.astype(ref.dtype)


def _mixer_in_prompt_kernel(x_ref, g_ref, w_ref, wt_ref, cost_ref, sint_ref, cw_ref, gc_ref,
                            qt_ref, kt_ref, kb_ref, v_ref, vt_ref, c_ref, cs_ref, carry_ref, *,
                            tiles_per_seq):
    tm = x_ref.shape[0]

    @pl.when(pl.program_id(0) % tiles_per_seq == 0)
    def _():
        carry_ref[...] = jnp.zeros_like(carry_ref)

    xn = _rms(x_ref[...], g_ref[...]).astype(BF16)

    def fix_history(row, um1, um2):
        prev0 = carry_ref[SUBLANES - 2:SUBLANES - 1, :]
        prev1 = carry_ref[SUBLANES - 1:SUBLANES, :]
        return (jnp.where(row == 0, prev1, um1),
                jnp.where(row == 0, prev0, jnp.where(row == 1, prev1, um2)))

    c, u = _conv_branch(xn, w_ref, cw_ref, gc_ref, fix_history)
    c_ref[...] = c
    carry_ref[...] = u[tm - SUBLANES:tm, :]
    cs_ref[...] = u[tm - (CONV_W - 1):tm, :]

    cos_t = cost_ref[...]
    sin_t = sint_ref[...]
    _rope_cols(_dot_nt(wt_ref[0:D_ATTN, :], xn), cos_t, sin_t, [(qt_ref, ATTN_SCALE)])
    _rope_cols(_dot_nt(wt_ref[D_ATTN:2 * D_ATTN, :], xn), cos_t, sin_t, [(kt_ref, 1.0)])
    kb_ref[...] = kt_ref[...].T.astype(kb_ref.dtype)
    v = _dot(xn, w_ref[:, 2 * D_ATTN:3 * D_ATTN])
    _store_heads(v_ref, v)
    vt_ref[...] = v.T.astype(vt_ref.dtype)


def _mixer_in_prompt(x, g, w_in, w_qk_t, tabs, conv_w, g_conv, *, batch, seq):
    n, d = x.shape
    dc = conv_w.shape[1]
    tm = min(ROW_TILE, seq)
    assert seq % tm == 0
    tps = seq // tm
    cos_t, sin_t = tabs
    row = lambda i: (i, 0)
    fixed = lambda i: (0, 0)
    tile_t = pl.BlockSpec((None, None, D_ATTN, tm), lambda i: (i // tps, i % tps, 0, 0))
    tile_t_shape = jax.ShapeDtypeStruct((batch, tps, D_ATTN, tm), BF16)
    return pl.pallas_call(
        functools.partial(_mixer_in_prompt_kernel, tiles_per_seq=tps),
        grid=(n // tm,),
        in_specs=[pl.BlockSpec((tm, d), row), pl.BlockSpec((1, d), fixed),
                  pl.BlockSpec(w_in.shape, fixed), pl.BlockSpec(w_qk_t.shape, fixed),
                  pl.BlockSpec((DH // 2, tm), lambda i: (0, i % tps)),
                  pl.BlockSpec((DH // 2, tm), lambda i: (0, i % tps)),
                  pl.BlockSpec((CONV_W, dc), fixed), pl.BlockSpec((1, dc), fixed)],
        out_specs=[tile_t,
                   pl.BlockSpec((None, D_ATTN, tm), lambda i: (i // tps, 0, i % tps)),
                   pl.BlockSpec((tm, D_ATTN), row),
                   pl.BlockSpec((tm, H_A, HEAD_W), lambda i: (i, 0, 0)),
                   tile_t,
                   pl.BlockSpec((tm, dc), row),
                   pl.BlockSpec((None, CONV_W - 1, dc), lambda i: (i // tps, 0, 0))],
        out_shape=[tile_t_shape,
                   jax.ShapeDtypeStruct((batch, D_ATTN, seq), F32),
                   jax.ShapeDtypeStruct((n, D_ATTN), BF16),
                   jax.ShapeDtypeStruct((n, H_A, HEAD_W), F32),
                   tile_t_shape,
                   jax.ShapeDtypeStruct((n, dc), BF16),
                   jax.ShapeDtypeStruct((batch, CONV_W - 1, dc), F32)],
        scratch_shapes=[pltpu.VMEM((SUBLANES, dc), F32)],
        compiler_params=_params(1),
        name="mixer_in_prompt",
    )(x, g, w_in, w_qk_t, cos_t, sin_t, conv_w, g_conv)


def _mixer_in_sample_kernel(x_ref, g_ref, w_ref, cos_ref, sin_ref, cw_ref, gc_ref, p1_ref, p2_ref,
                            q_ref, k_ref, v_ref, c_ref, u_ref, *, seq_rows):
    xn = _rms(x_ref[...], g_ref[...]).astype(BF16)
    cos = cos_ref[...]
    sin = sin_ref[...]
    for c, r in enumerate(_rope_rows(_dot(xn, w_ref[:, 0:D_ATTN]), cos, sin)):
        q_ref[:, c * LANES:(c + 1) * LANES] = r * ATTN_SCALE
    for c, r in enumerate(_rope_rows(_dot(xn, w_ref[:, D_ATTN:2 * D_ATTN]), cos, sin)):
        k_ref[:, c * LANES:(c + 1) * LANES] = r
    v_ref[...] = _dot(xn, w_ref[:, 2 * D_ATTN:3 * D_ATTN])

    def fix_history(row, um1, um2):
        t = row % seq_rows
        return jnp.where(t == 0, p1_ref[...], um1), jnp.where(t < 2, p2_ref[...], um2)

    c, u = _conv_branch(xn, w_ref, cw_ref, gc_ref, fix_history)
    c_ref[...] = c
    u_ref[...] = u


def _mixer_in_sample(x, g, w_in, tabs, conv_w, g_conv, hist, *, seq_rows):
    n, d = x.shape
    dc = conv_w.shape[1]
    tm = min(ROW_TILE, n)
    cos, sin = tabs
    row = lambda i: (i, 0)
    fixed = lambda i: (0, 0)
    act = pl.BlockSpec((tm, D_ATTN), row)
    conv = pl.BlockSpec((tm, dc), row)
    return pl.pallas_call(
        functools.partial(_mixer_in_sample_kernel, seq_rows=seq_rows),
        grid=(n // tm,),
        in_specs=[pl.BlockSpec((tm, d), row), pl.BlockSpec((1, d), fixed),
                  pl.BlockSpec(w_in.shape, fixed),
                  pl.BlockSpec((tm, LANES), fixed), pl.BlockSpec((tm, LANES), fixed),
                  pl.BlockSpec((CONV_W, dc), fixed), pl.BlockSpec((1, dc), fixed), conv, conv],
        out_specs=[act, act, act, conv, conv],
        out_shape=[jax.ShapeDtypeStruct((n, D_ATTN), F32)] * 3
                  + [jax.ShapeDtypeStruct((n, dc), BF16), jax.ShapeDtypeStruct((n, dc), F32)],
        compiler_params=_params(1),
        name="mixer_in_sample",
    )(x, g, w_in, cos, sin, conv_w, g_conv, *hist)


def _head_out(a0, l0, a1, l1, lam, g, lam0):
    o = a0 / l0 - lam * (a1 / l1)
    return _rms(o, g) * (1.0 - lam0)


def _prompt_attn_kernel(qt_ref, k_ref, vt_ref, lq1, lk1, lq2, lk2, gs_ref, o_ref,
                        qt2_ref, s_ref, vx_ref, m_ref, acc_ref, *, lam0):
    nk, _, tq = qt_ref.shape
    vx_ref[:, 0:HEAD_W, :] = vt_ref[...]
    vx_ref[:, HEAD_W:, :] = jnp.ones((nk, ONES_ROWS, tq), vx_ref.dtype)
    lam = _diff_lambda(lq1, lk1, lq2, lk2, lam0)

    def scores(qi, j, sub):
        k = k_ref[pl.ds(pl.multiple_of(j * tq, tq), tq), :]
        s_ref[sub] = _dot(k, qt2_ref[qi % 2, sub])

    def softmax_pv(qi, j, sub, masked):
        st = s_ref[sub]
        if masked:
            key = lax.broadcasted_iota(jnp.int32, st.shape, 0)
            qry = lax.broadcasted_iota(jnp.int32, st.shape, 1)
            st = jnp.where(key <= qry, st, -jnp.inf)
        m_prev = m_ref[qi % 2, sub]
        m_new = jnp.maximum(m_prev, jnp.max(st, axis=0, keepdims=True))
        p = jnp.exp(st - m_new)
        alpha = jnp.exp(m_prev - m_new)
        acc_ref[qi % 2, sub] = alpha * acc_ref[qi % 2, sub] + _dot(vx_ref[j], p.astype(BF16))
        m_ref[qi % 2, sub] = m_new

    def head(qi):
        qt = qt_ref[qi]
        d_row = lax.broadcasted_iota(jnp.int32, qt.shape, 0)
        qt2_ref[qi % 2, 0] = jnp.where(d_row < DH, qt, jnp.zeros_like(qt))
        qt2_ref[qi % 2, 1] = jnp.where(d_row >= DH, qt, jnp.zeros_like(qt))
        m_ref[qi % 2] = jnp.full(m_ref.shape[1:], -jnp.inf, F32)
        acc_ref[qi % 2] = jnp.zeros(acc_ref.shape[1:], F32)
        scores(qi, 0, 0)

    def finish(qi):
        a0 = acc_ref[qi % 2, 0]
        a1 = acc_ref[qi % 2, 1]
        ot = (a0[0:HEAD_W] / a0[HEAD_W:HEAD_W + 1]
              - lam * (a1[0:HEAD_W] / a1[HEAD_W:HEAD_W + 1]))
        o_ref[qi * tq:(qi + 1) * tq, :] = (_rms(ot.T, gs_ref[...])
                                           * (1.0 - lam0)).astype(o_ref.dtype)

    head(0)
    for qi in range(nk):
        def body(j, carry, qi=qi):
            scores(qi, j, 1)
            softmax_pv(qi, j, 0, False)
            scores(qi, j + 1, 0)
            softmax_pv(qi, j, 1, False)
            return carry

        lax.fori_loop(0, qi, body, 0)
        scores(qi, qi, 1)
        softmax_pv(qi, qi, 0, True)
        if qi + 1 < nk:
            head(qi + 1)
        softmax_pv(qi, qi, 1, True)
        finish(qi)


def _prompt_attn(qt, kb, vt, lams, g_subln, *, lam0):
    batch, nk, _, tk = qt.shape
    seq = nk * tk
    vec = pl.BlockSpec((1, DH), lambda b, h: (0, 0))
    tiles_t = pl.BlockSpec((None, nk, HEAD_W, tk), lambda b, h: (b, 0, h, 0))
    return pl.pallas_call(
        functools.partial(_prompt_attn_kernel, lam0=lam0),
        grid=(batch, H_A),
        in_specs=[tiles_t, pl.BlockSpec((seq, HEAD_W), lambda b, h: (b, h)), tiles_t,
                  vec, vec, vec, vec, pl.BlockSpec((1, HEAD_W), lambda b, h: (0, 0))],
        out_specs=pl.BlockSpec((seq, HEAD_W), lambda b, h: (b, h)),
        out_shape=jax.ShapeDtypeStruct((batch * seq, D_ATTN), BF16),
        scratch_shapes=[pltpu.VMEM((2, 2, HEAD_W, tk), BF16),
                        pltpu.VMEM((2, tk, tk), F32),
                        pltpu.VMEM((nk, HEAD_W + ONES_ROWS, tk), BF16),
                        pltpu.VMEM((2, 2, 1, tk), F32),
                        pltpu.VMEM((2, 2, HEAD_W + ONES_ROWS, tk), F32)],
        compiler_params=_params(2),
        name="prompt_attn",
    )(qt, kb, vt, *lams, g_subln)


def _paged_ops(pt_ref, kpool, vpool, kbuf, vbuf, sem, qbd_ref, m_ref, l_ref, acc_ref, *, pages, t):
    rows, dq = qbd_ref.shape
    page = kbuf.shape[-1]

    def page_copies(g, slot):
        out = []
        for j in range(pages):
            pid = pt_ref[g * pages + j]
            out.append(pltpu.make_async_copy(kpool.at[pid], kbuf.at[slot, j], sem.at[0, slot, j]))
            out.append(pltpu.make_async_copy(vpool.at[pid], vbuf.at[slot, j], sem.at[1, slot, j]))
        return out

    def start(g, slot):
        for cp in page_copies(g, slot):
            cp.start()

    def wait(g, slot):
        for cp in page_copies(g, slot):
            cp.wait()

    def softmax_stats(s):
        m_prev = m_ref[...]
        m_new = jnp.maximum(m_prev, jnp.max(s, axis=-1, keepdims=True))
        alpha = jnp.exp(m_prev - m_new)
        p = jnp.exp(s - m_new)
        l_ref[...] = alpha * l_ref[...] + jnp.sum(p, axis=-1, keepdims=True)
        m_ref[...] = m_new
        return p.astype(BF16), alpha

    def pv_update(pb, alpha, value):
        pv = []
        for h in range(0, H_A, 2):
            both = _dot(pb[2 * h * t:2 * (h + 2) * t, :],
                        jnp.concatenate([value(h), value(h + 1)], axis=1))
            pv += [both[0:2 * t, 0:HEAD_W], both[2 * t:4 * t, HEAD_W:]]
        acc_ref[...] = alpha * acc_ref[...] + jnp.concatenate(pv, axis=0)

    def update(s, value):
        pb, alpha = softmax_stats(s)
        pv_update(pb, alpha, value)

    def begin_seq(q):
        qt = jnp.concatenate([q] * N_SUB, axis=0)
        r_i = lax.broadcasted_iota(jnp.int32, (rows, dq), 0)
        c_i = lax.broadcasted_iota(jnp.int32, (rows, dq), 1)
        qbd_ref[...] = jnp.where((r_i // t) == (c_i // DH), qt, jnp.zeros_like(qt)).astype(BF16)
        m_ref[...] = jnp.full_like(m_ref, -jnp.inf)
        l_ref[...] = jnp.zeros_like(l_ref)
        acc_ref[...] = jnp.zeros_like(acc_ref)

    def chunk_scores(slot):
        kt = jnp.concatenate([kbuf[slot, j].reshape(dq, page).astype(BF16)
                              for j in range(pages)], axis=1)
        return _dot(qbd_ref[...], kt)

    def chunk_values(slot):
        return lambda h: jnp.concatenate(
            [vbuf[slot, j, pl.ds(h, page, stride=H_A), :].astype(BF16) for j in range(pages)],
            axis=0)

    def end_seq(k_new, v_new, lam, g, lam0):
        pad = jnp.zeros((LANES - t, dq), F32)
        kn = jnp.concatenate([k_new, pad], axis=0).astype(BF16)
        vn = jnp.concatenate([v_new, pad], axis=0).astype(BF16)
        s = _dot_nt(qbd_ref[...], kn)
        key_t = lax.broadcasted_iota(jnp.int32, s.shape, 1)
        qry_t = lax.broadcasted_iota(jnp.int32, s.shape, 0) % t
        update(jnp.where(key_t <= qry_t, s, -jnp.inf),
               lambda h: vn[:, h * HEAD_W:(h + 1) * HEAD_W])
        acc = acc_ref[...]
        l = l_ref[...]
        return jnp.concatenate(
            [_head_out(acc[2 * h * t:(2 * h + 1) * t], l[2 * h * t:(2 * h + 1) * t],
                       acc[(2 * h + 1) * t:(2 * h + 2) * t], l[(2 * h + 1) * t:(2 * h + 2) * t],
                       lam, g, lam0) for h in range(H_A)], axis=1)

    return start, wait, begin_seq, chunk_scores, softmax_stats, pv_update, chunk_values, end_seq


def _mixer_out_kernel(x_ref, o_ref, c_ref, wo_ref, g_ref, wq_ref, x1_ref, qm_ref, *, scale):
    da = o_ref.shape[1]
    x1 = (x_ref[...] + _dot(o_ref[...].astype(BF16), wo_ref[0:da, :])
          + _dot(c_ref[...], wo_ref[da:, :]))
    x1_ref[...] = x1
    xn = _rms(x1, g_ref[...]).astype(BF16)
    qm_ref[...] = (_dot(xn, wq_ref[...]) * scale).astype(qm_ref.dtype)


def _mixer_out(x, o, c, w_out, g_cross, w_q, *, scale, act_dtype):
    n, d = x.shape
    tm = min(ROW_TILE, n)
    row = lambda i: (i, 0)
    fixed = lambda i: (0, 0)
    return pl.pallas_call(
        functools.partial(_mixer_out_kernel, scale=scale),
        grid=(n // tm,),
        in_specs=[pl.BlockSpec((tm, d), row), pl.BlockSpec((tm, o.shape[1]), row),
                  pl.BlockSpec((tm, c.shape[1]), row), pl.BlockSpec((d, d), fixed),
                  pl.BlockSpec((1, d), fixed), pl.BlockSpec((d, d), fixed)],
        out_specs=[pl.BlockSpec((tm, d), row)] * 2,
        out_shape=[jax.ShapeDtypeStruct((n, d), F32), jax.ShapeDtypeStruct((n, d), act_dtype)],
        compiler_params=_params(1),
        name="mixer_out",
    )(x, o, c, w_out, g_cross, w_q)


def _cross_attn_kernel(q_ref, mk_ref, mv_ref, o_ref, *, seqs):
    dh = q_ref.shape[1] // H_MEM
    tq = q_ref.shape[0] // seqs
    mem_rows = mk_ref.shape[0] // seqs

    def head(ref, i, h):
        if ref.shape[1] == q_ref.shape[1]:
            return ref[i * mem_rows:(i + 1) * mem_rows, h * dh:(h + 1) * dh].astype(BF16)
        chunks = dh // LANES
        step = chunks * H_MEM
        return jnp.concatenate(
            [ref[pl.ds(i * mem_rows + c * H_MEM + h, mem_rows // step, stride=step), :]
             for c in range(chunks)], axis=1).astype(BF16)

    pairs = [(i, h) for i in range(seqs) for h in range(H_MEM)]
    block = lambda i, h: (slice(i * tq, (i + 1) * tq), slice(h * dh, (h + 1) * dh))
    s = [_dot_nt(q_ref[block(i, h)].astype(BF16), head(mk_ref, i, h)) for i, h in pairs]
    p = [jnp.exp(x - jnp.max(x, axis=-1, keepdims=True)) for x in s]
    l = [jnp.sum(x, axis=-1, keepdims=True) for x in p]
    for (i, h), x, y in zip(pairs, p, l):
        o_ref[block(i, h)] = (_dot(x.astype(BF16), head(mv_ref, i, h)) / y).astype(o_ref.dtype)


def _cross_attn(qm, mk, mv, *, rows_per_seq):
    n, d = qm.shape
    n_seq = n // rows_per_seq
    mem_rows = mk.shape[0] // n_seq
    if rows_per_seq >= ROW_TILE:
        tm, seqs = ROW_TILE, 1
        tps = rows_per_seq // tm
        mem = pl.BlockSpec((mem_rows, mk.shape[1]), lambda i: (i // tps, 0))
    else:
        seqs = min(SHORT_SEQS_PER_STEP, n_seq)
        tm = seqs * rows_per_seq
        mem = pl.BlockSpec((seqs * mem_rows, mk.shape[1]), lambda i: (i, 0))
    assert n % tm == 0
    return pl.pallas_call(
        functools.partial(_cross_attn_kernel, seqs=seqs),
        grid=(n // tm,),
        in_specs=[pl.BlockSpec((tm, d), lambda i: (i, 0)), mem, mem],
        out_specs=pl.BlockSpec((tm, d), lambda i: (i, 0)),
        out_shape=jax.ShapeDtypeStruct((n, d), qm.dtype),
        compiler_params=_params(1),
        name="cross_attn",
    )(qm, mk, mv)


def _mlp_kernel(x_ref, oc_ref, wo_ref, g_ref, wu_ref, wd_ref, gf_ref, y_ref, *, ff_chunk):
    x2 = x_ref[...] + _dot(oc_ref[...].astype(BF16), wo_ref[...])
    xn = _rms(x2, g_ref[...]).astype(BF16)
    acc = x2
    for c in range(wu_ref.shape[1] // ff_chunk):
        sl = slice(c * ff_chunk, (c + 1) * ff_chunk)
        h = jnp.maximum(_dot(xn, wu_ref[:, sl]), 0.0)
        acc = acc + _dot((h * h).astype(BF16), wd_ref[sl, :])
    y_ref[...] = _rms(acc, gf_ref[...])


def _mlp(x1, oc, w_o, g_mlp, w_up, w_down, g_final):
    n, d = x1.shape
    dff = w_up.shape[1]
    tm = min(ROW_TILE, n)
    row = lambda i: (i, 0)
    fixed = lambda i: (0, 0)
    return pl.pallas_call(
        functools.partial(_mlp_kernel, ff_chunk=min(1024, dff)),
        grid=(n // tm,),
        in_specs=[pl.BlockSpec((tm, d), row), pl.BlockSpec((tm, d), row),
                  pl.BlockSpec((d, d), fixed), pl.BlockSpec((1, d), fixed),
                  pl.BlockSpec((d, dff), fixed), pl.BlockSpec((dff, d), fixed),
                  pl.BlockSpec((1, d), fixed)],
        out_specs=pl.BlockSpec((tm, d), row),
        out_shape=jax.ShapeDtypeStruct((n, d), F32),
        compiler_params=_params(1),
        name="mlp",
    )(x1, oc, w_o, g_mlp, w_up, w_down, g_final)


def _mlp_paged_kernel(pt_ref, x_ref, oc_ref, wo_ref, g_ref, wu_ref, wd_ref, gf_ref,
                      q_ref, kn_ref, vn_ref, lq1, lk1, lq2, lk2, gs_ref, kpool, vpool,
                      y_ref, o_ref, kbuf, vbuf, sem, qbd_ref, m_ref, l_ref, acc_ref, *,
                      ff_chunk, pages, chunks_per_seq, lam0):
    i = pl.program_id(0)
    t = q_ref.shape[0]
    (start, wait, begin_seq, chunk_scores, softmax_stats, pv_update, chunk_values,
     end_seq) = _paged_ops(pt_ref, kpool, vpool, kbuf, vbuf, sem, qbd_ref, m_ref, l_ref, acc_ref,
                           pages=pages, t=t)
    g0 = i * chunks_per_seq

    @pl.when(i == 0)
    def _():
        start(0, 0)

    n_ff = wu_ref.shape[1] // ff_chunk
    ff_done = 0
    begin_seq(q_ref[...])
    xn = acc = None

    def up(sl):
        h = jnp.maximum(_dot(xn, wu_ref[:, sl]), 0.0)
        return (h * h).astype(BF16)

    for c in range(chunks_per_seq):
        if c + 1 < chunks_per_seq:
            start(g0 + c + 1, (c + 1) % 2)
        else:
            @pl.when(i + 1 < pl.num_programs(0))
            def _():
                start(g0 + chunks_per_seq, 0)
        wait(g0 + c, c % 2)
        sc = chunk_scores(c % 2)
        if c == 0:
            x2 = x_ref[...] + _dot(oc_ref[...].astype(BF16), wo_ref[...])
            xn = _rms(x2, g_ref[...]).astype(BF16)
            acc = x2
        todo = []
        while ff_done < (c + 1) * n_ff // chunks_per_seq:
            todo.append(slice(ff_done * ff_chunk, (ff_done + 1) * ff_chunk))
            ff_done += 1
        hs = [(up(sl), sl) for sl in todo]
        pb, alpha = softmax_stats(sc)
        for hb, sl in hs:
            acc = acc + _dot(hb, wd_ref[sl, :])
        pv_update(pb, alpha, chunk_values(c % 2))
    lam = _diff_lambda(lq1, lk1, lq2, lk2, lam0)
    o_ref[...] = end_seq(kn_ref[...], vn_ref[...], lam, gs_ref[...], lam0).astype(o_ref.dtype)
    y_ref[...] = _rms(acc, gf_ref[...])


def _mlp_paged(x1, oc, w_o, g_mlp, w_up, w_down, g_final,
               q, k_new, v_new, lams, g_subln, pool_kt, pool_v, page_table, *, lam0):
    n, d = x1.shape
    dff = w_up.shape[1]
    tm = min(ROW_TILE, n)
    n_seq, n_pages = page_table.shape
    t = q.shape[0] // n_seq
    page = pool_kt.shape[3]
    pages = min(PAGES_PER_STEP, n_pages)
    cps = n_pages // pages
    assert n_seq == n // tm, "one sample sequence per MLP row tile"
    assert n_pages % pages == 0 and cps % 2 == 0
    row = lambda i, pt: (i, 0)
    fixed = lambda i, pt: (0, 0)
    once = dict(pipeline_mode=pl.Buffered(1))
    seq = pl.BlockSpec((t, D_ATTN), row)
    vec = pl.BlockSpec((1, DH), fixed)
    rows = N_SUB * t
    grid_spec = pltpu.PrefetchScalarGridSpec(
        num_scalar_prefetch=1,
        grid=(n // tm,),
        in_specs=[pl.BlockSpec((tm, d), row), pl.BlockSpec((tm, d), row),
                  pl.BlockSpec((d, d), fixed, **once), pl.BlockSpec((1, d), fixed),
                  pl.BlockSpec((d, dff), fixed, **once), pl.BlockSpec((dff, d), fixed, **once),
                  pl.BlockSpec((1, d), fixed),
                  seq, seq, seq, vec, vec, vec, vec, pl.BlockSpec((1, HEAD_W), fixed),
                  pl.BlockSpec(memory_space=pl.ANY), pl.BlockSpec(memory_space=pl.ANY)],
        out_specs=[pl.BlockSpec((tm, d), row), seq],
        scratch_shapes=[pltpu.VMEM((2, pages, N_SUB, DH, page), F32),
                        pltpu.VMEM((2, pages, page * H_A, HEAD_W), F32),
                        pltpu.SemaphoreType.DMA((2, 2, pages)),
                        pltpu.VMEM((rows, D_ATTN), BF16),
                        pltpu.VMEM((rows, 1), F32), pltpu.VMEM((rows, 1), F32),
                        pltpu.VMEM((rows, HEAD_W), F32)],
    )
    return pl.pallas_call(
        functools.partial(_mlp_paged_kernel, ff_chunk=min(FUSED_FF_CHUNK, dff), pages=pages,
                          chunks_per_seq=cps, lam0=lam0),
        grid_spec=grid_spec,
        out_shape=[jax.ShapeDtypeStruct((n, d), F32),
                   jax.ShapeDtypeStruct((n_seq * t, D_ATTN), F32)],
        compiler_params=_params(1),
        name="mlp_paged",
    )(page_table.reshape(-1), x1, oc, w_o, g_mlp, w_up, w_down, g_final,
      q, k_new, v_new, *lams, g_subln, pool_kt, pool_v)


def _rope_angles(pos):
    half = DH // 2
    inv = jnp.exp(jnp.arange(half, dtype=F32) * (-2.0 * math.log(ROPE_THETA) / DH))
    ang = pos.astype(F32)[:, None] * inv[None, :]
    return jnp.cos(ang), jnp.sin(ang)


def _rope_row_tables(pos, reps):
    cos, sin = _rope_angles(pos)
    cos = jnp.tile(jnp.concatenate([cos, cos], axis=-1), (reps, LANES // DH))
    sin = jnp.tile(jnp.concatenate([-sin, sin], axis=-1), (reps, LANES // DH))
    return cos, sin


def kernel(x_prompt, x_sample, mem_prompt, cache_k, cache_v, state_conv, cache_mem_k, cache_mem_v, page_table, g_mix, w_in, lambda_q1, lambda_k1, lambda_q2, lambda_k2, g_subln, conv_w, g_conv, w_out, g_cross, g_mem, w_q_mem, w_k_mem, w_v_mem, w_o_mem, g_mlp, w_up, w_down, g_final):
    assert w_in.shape[0] == 1, "single-layer trunk: the final RMSNorm is fused into the MLP kernel"
    b_p, s_p, d = x_prompt.shape
    b_s, t_s, _ = x_sample.shape
    n_mem = mem_prompt.shape[1]
    dc = conv_w.shape[-1]
    dh_mem = d // H_MEM
    mem_scale = dh_mem ** -0.5
    past_len = page_table.shape[1] * cache_k.shape[2]
    assert t_s >= CONV_W - 1 and (b_s * t_s) % SUBLANES == 0

    pos_p = jnp.arange(s_p)
    tabs_p = tuple(a.T for a in _rope_angles(pos_p))
    tm_s = min(ROW_TILE, b_s * t_s)
    tabs_s = _rope_row_tables(past_len + jnp.arange(t_s), tm_s // t_s)

    xp = x_prompt.reshape(b_p * s_p, d)
    xs = x_sample.reshape(b_s * t_s, d)
    mem = mem_prompt.reshape(b_p * n_mem, d)
    row = lambda a: a.reshape(1, -1)
    l = 0
    lam0 = _lambda_init(l)
    lams = [row(a[l]) for a in (lambda_q1, lambda_k1, lambda_q2, lambda_k2)]
    w_in_b = w_in[l].astype(BF16)
    w_qk_t = w_in[l][:, 0:2 * D_ATTN].T.astype(BF16)
    w_out_b = w_out[l].astype(BF16)
    w_q_b = w_q_mem[l].astype(BF16)
    w_o_b = w_o_mem[l].astype(BF16)
    w_up_b = w_up[l].astype(BF16)
    w_down_b = w_down[l].astype(BF16)

    def pre_mlp(x, o, c, mk, mv, rows_per_seq, act_dtype):
        x1, qm = _mixer_out(x, o, c, w_out_b, row(g_cross[l]), w_q_b, scale=mem_scale,
                            act_dtype=act_dtype)
        return x1, _cross_attn(qm, mk, mv, rows_per_seq=rows_per_seq)

    mlp_weights = (w_o_b, row(g_mlp[l]), w_up_b, w_down_b, row(g_final))

    mk, mv, mkb, mvb = _mem_kv(mem, row(g_mem[l]), w_k_mem[l].astype(BF16),
                               w_v_mem[l].astype(BF16))
    qt, kt, kb, vp, vt, c, conv_p = _mixer_in_prompt(
        xp, row(g_mix[l]), w_in_b, w_qk_t, tabs_p, conv_w[l], row(g_conv[l]), batch=b_p, seq=s_p)
    o = _prompt_attn(qt, kb, vt, lams, row(g_subln[l]), lam0=lam0)
    x1p, ocp = pre_mlp(xp, o, c, mkb, mvb, s_p, BF16)
    kp = kt.reshape(b_p, N_SUB, DH, s_p).transpose(0, 3, 1, 2)

    prev = state_conv[l]
    p1 = jnp.pad(prev[:, 1:2], ((0, 0), (0, t_s - 1), (0, 0))).reshape(b_s * t_s, dc)
    p2 = jnp.pad(prev, ((0, 0), (0, t_s - 2), (0, 0))).reshape(b_s * t_s, dc)
    q, ks, vs, c, u = _mixer_in_sample(xs, row(g_mix[l]), w_in_b, tabs_s, conv_w[l],
                                       row(g_conv[l]), (p1, p2), seq_rows=t_s)

    yp, o = _mlp_paged(x1p, ocp, *mlp_weights, q, ks, vs, lams, row(g_subln[l]),
                       cache_k[l].transpose(0, 2, 3, 1),
                       cache_v[l].reshape(cache_v.shape[1], -1, HEAD_W), page_table, lam0=lam0)

    def lane_rows(m):
        m = m.reshape(b_s * n_mem, H_MEM, dh_mem // LANES, LANES)
        return m.transpose(0, 2, 1, 3).reshape(-1, LANES)

    x1s, ocs = pre_mlp(xs, o, c, lane_rows(cache_mem_k[l]), lane_rows(cache_mem_v[l]), t_s, F32)
    ys = _mlp(x1s, ocs, *mlp_weights)
    conv_s = u.reshape(b_s, t_s, dc)[:, t_s - (CONV_W - 1):]

    return (yp.reshape(b_p, s_p, d), ys.reshape(b_s, t_s, d),
            kp[None], vp.reshape(1, b_p, s_p, H_A, HEAD_W), conv_p[None],
            mk.reshape(1, b_p, n_mem, H_MEM, dh_mem), mv.reshape(1, b_p, n_mem, H_MEM, dh_mem),
            ks.reshape(1, b_s, t_s, N_SUB, DH), vs.reshape(1, b_s, t_s, H_A, HEAD_W),
            conv_s[None])
```
